```python
import math
import jax, jax.numpy as jnp
from jax import lax
import numpy as np

D_MODEL = 2048
BATCH = 1
SEQ = 8192
DEPTH = 2
DEC_BATCH = 32
DEC_SEQ = 8
PAST_LEN = 8192
PAGE_SIZE = 128

H_A = D_MODEL // 256
DH_A = 64
DV_A = 2 * DH_A
H_B = D_MODEL // 256
DH_B = 128
C_GROUPS = D_MODEL // 256
C_GROUP_DIM = 128
C_WIDTH = C_GROUPS * C_GROUP_DIM
CHUNK = 128
Q_BLOCK = 128
D_FF = -(-8 * D_MODEL // (3 * 256)) * 256
N_BRANCH = 3
IN_WIDTH = 2 * (2 * H_A * DH_A) + H_A * DV_A + 3 * H_B * DH_B + H_B + 2 * C_WIDTH
POOL_NUM = 5
POOL_DEN = 4
FORGET_BIAS = 3.0
EPS = 1e-6

kernel_name = "hybrid_gated_diffattn_fox_sgu_step"


def rms_norm(x, g):
    xf = x.astype(jnp.float32)
    y = xf * lax.rsqrt(jnp.mean(xf * xf, axis=-1, keepdims=True) + EPS)
    return (y * g.astype(jnp.float32)).astype(x.dtype)


def split_cols(p):
    widths = [2 * H_A * DH_A, 2 * H_A * DH_A, H_A * DV_A,
              H_B * DH_B, H_B * DH_B, H_B * DH_B, H_B, C_WIDTH, C_WIDTH]
    idx = np.cumsum(widths)[:-1].tolist()
    return jnp.split(p, idx, axis=-1)


def sweep_query_blocks(block_fn, n_q):
    if n_q > Q_BLOCK and n_q % Q_BLOCK == 0:
        starts = jnp.arange(n_q // Q_BLOCK, dtype=jnp.int32) * Q_BLOCK
        out = lax.map(lambda s: block_fn(s, Q_BLOCK), starts)
        out = jnp.moveaxis(out, 0, 1)
        return out.reshape((out.shape[0], n_q) + out.shape[3:])
    return block_fn(0, n_q)


def diff_attention(q, k, v, qpos, kpos, lam):
    slopes = jnp.exp2(-8.0 * jnp.arange(1, H_A + 1, dtype=jnp.float32) / H_A)
    scale = DH_A ** -0.5

    def block(start, size):
        qb = lax.dynamic_slice_in_dim(q, start, size, axis=1)
        pb = lax.dynamic_slice_in_dim(qpos, start, size, axis=0)
        dist = pb[:, None] - kpos[None, :]
        s = jnp.einsum('bqhmd,bkhmd->bhmqk', qb, k).astype(jnp.float32) * scale
        s = s - slopes[:, None, None, None] * dist.astype(jnp.float32)
        s = jnp.where(dist >= 0, s, -jnp.inf)
        p = jax.nn.softmax(s, axis=-1)
        a = p[:, :, 0] - lam * p[:, :, 1]
        return jnp.einsum('bhqk,bkhd->bqhd', a.astype(v.dtype), v)

    return sweep_query_blocks(block, q.shape[1])


def forgetting_attention(q, k, v, cum_logf, qpos, kpos):
    scale = DH_B ** -0.5
    lq, lk = q.shape[1], k.shape[1]
    cq = cum_logf[:, lk - lq:]
    ck = jnp.moveaxis(cum_logf, 1, 2)

    def block(start, size):
        qb = lax.dynamic_slice_in_dim(q, start, size, axis=1)
        pb = lax.dynamic_slice_in_dim(qpos, start, size, axis=0)
        cb = jnp.moveaxis(lax.dynamic_slice_in_dim(cq, start, size, axis=1), 1, 2)
        s = jnp.einsum('bqhd,bkhd->bhqk', qb, k).astype(jnp.float32) * scale
        s = s + cb[..., None] - ck[:, :, None, :]
        s = jnp.where(pb[:, None] >= kpos[None, :], s, -jnp.inf)
        p = jax.nn.softmax(s, axis=-1)
        return jnp.einsum('bhqk,bkhd->bqhd', p.astype(v.dtype), v)

    return sweep_query_blocks(block, lq)


def chunk_spatial_mix(v, w_s, b_s):
    bsz, length = v.shape[0], v.shape[1]
    n_chunks = -(-length // CHUNK)
    pad = n_chunks * CHUNK - length
    vp = jnp.pad(v, ((0, 0), (0, pad), (0, 0), (0, 0)))
    vp = vp.reshape(bsz, n_chunks, CHUNK, C_GROUPS, C_GROUP_DIM)
    w = jnp.where(jnp.tril(jnp.ones((CHUNK, CHUNK), dtype=bool)), w_s, 0.0)
    mixed = jnp.einsum('gts,bnsgc->bntgc', w.astype(vp.dtype), vp)
    mixed = mixed + jnp.swapaxes(b_s, 0, 1)[None, None, :, :, None]
    return mixed.reshape(bsz, n_chunks * CHUNK, C_GROUPS, C_GROUP_DIM)[:, :length]


def gather_pages(cache_l, page_table):
    n_seq, n_pages = page_table.shape
    pages = cache_l[page_table]
    return pages.reshape((n_seq, n_pages * PAGE_SIZE) + cache_l.shape[2:])


def trunk_layer(x, qpos, kpos, past, lp, lam_init):
    (g_mix, w_in, b_forget, g_q_a, g_k_a, lam_q1, lam_k1, lam_q2, lam_k2, g_subln,
     g_q_b, g_k_b, g_sgu, w_spatial, b_spatial, w_branch_a, w_branch_b, w_branch_c,
     w_gate, b_gate, w_out, g_ffn, w_ffn_gate, w_ffn_up, w_ffn_down) = lp
    bsz, seq_len, _ = x.shape
    h = rms_norm(x, g_mix)
    qa, ka, va, qb, kb, vb, fb, uc, vc = split_cols(h @ w_in)
    qa = rms_norm(qa.reshape(bsz, seq_len, H_A, 2, DH_A), g_q_a)
    ka = rms_norm(ka.reshape(bsz, seq_len, H_A, 2, DH_A), g_k_a)
    va = va.reshape(bsz, seq_len, H_A, DV_A)
    qb = rms_norm(qb.reshape(bsz, seq_len, H_B, DH_B), g_q_b)
    kb = rms_norm(kb.reshape(bsz, seq_len, H_B, DH_B), g_k_b)
    vb = vb.reshape(bsz, seq_len, H_B, DH_B)
    logf = jax.nn.log_sigmoid((fb + b_forget).astype(jnp.float32))

    if past is None:
        ka_all, va_all, kb_all, vb_all, logf_all = ka, va, kb, vb, logf
    else:
        pka, pva, pkb, pvb, plogf = past
        cat = lambda p, n: jnp.concatenate([p.astype(n.dtype), n], axis=1)
        ka_all, va_all = cat(pka, ka), cat(pva, va)
        kb_all, vb_all = cat(pkb, kb), cat(pvb, vb)
        logf_all = cat(plogf, logf)

    lam = (jnp.exp(jnp.sum(lam_q1 * lam_k1).astype(jnp.float32))
           - jnp.exp(jnp.sum(lam_q2 * lam_k2).astype(jnp.float32)) + lam_init)
    oa = diff_attention(qa, ka_all, va_all, qpos, kpos, lam)
    oa = rms_norm(oa, g_subln) * (1.0 - lam_init)
    cum = lax.cumsum(logf_all, axis=1)
    ob = forgetting_attention(qb, kb_all, vb_all, cum, qpos, kpos)
    vc = rms_norm(jax.nn.gelu(vc, approximate=False), g_sgu)
    mixed = chunk_spatial_mix(vc.reshape(bsz, seq_len, C_GROUPS, C_GROUP_DIM), w_spatial, b_spatial)
    oc = jax.nn.gelu(uc, approximate=False) * mixed.reshape(bsz, seq_len, C_WIDTH)

    ya = oa.reshape(bsz, seq_len, H_A * DV_A) @ w_branch_a
    yb = ob.reshape(bsz, seq_len, H_B * DH_B) @ w_branch_b
    yc = oc @ w_branch_c
    ga, gb, gc = jnp.split(jax.nn.sigmoid(h @ w_gate + b_gate), N_BRANCH, axis=-1)
    x = x + (ga * ya + gb * yb + gc * yc) @ w_out
    h2 = rms_norm(x, g_ffn)
    x = x + (jax.nn.silu(h2 @ w_ffn_gate) * (h2 @ w_ffn_up)) @ w_ffn_down
    return x, (ka, va, kb, vb, logf, vc)


def setup_inputs(seed: int = 0) -> dict:
    key = jax.random.key(seed)
    k = jax.random.split(key, 40)
    nrm = lambda kk, shape, scale=1.0: scale * jax.random.normal(kk, shape, jnp.float32)
    gain = lambda kk, shape: 1.0 + 0.02 * jax.random.normal(kk, shape, jnp.float32)
    n_pages = PAST_LEN // PAGE_SIZE
    n_used = DEC_BATCH * n_pages
    n_phys = -(-n_used * POOL_NUM // POOL_DEN)
    page_table = jax.random.permutation(k[0], n_phys)[:n_used].reshape(DEC_BATCH, n_pages).astype(jnp.int32)
    pool = (DEPTH, n_phys, PAGE_SIZE)
    return {
        "x_prompt": nrm(k[1], (BATCH, SEQ, D_MODEL)),
        "x_sample": nrm(k[2], (DEC_BATCH, DEC_SEQ, D_MODEL)),
        "cache_diff_k": nrm(k[3], pool + (H_A, 2, DH_A)),
        "cache_diff_v": nrm(k[4], pool + (H_A, DV_A)),
        "cache_fox_k": nrm(k[5], pool + (H_B, DH_B)),
        "cache_fox_v": nrm(k[6], pool + (H_B, DH_B)),
        "cache_fox_logf": jax.nn.log_sigmoid(FORGET_BIAS + nrm(k[7], pool + (H_B,))),
        "page_table": page_table,
        "g_mix": gain(k[8], (DEPTH, D_MODEL)),
        "w_in": nrm(k[9], (DEPTH, D_MODEL, IN_WIDTH), D_MODEL ** -0.5),
        "b_forget": FORGET_BIAS + nrm(k[10], (DEPTH, H_B), 0.1),
        "g_q_a": gain(k[11], (DEPTH, DH_A)),
        "g_k_a": gain(k[12], (DEPTH, DH_A)),
        "lam_q1": nrm(k[13], (DEPTH, DH_A), 0.1),
        "lam_k1": nrm(k[14], (DEPTH, DH_A), 0.1),
        "lam_q2": nrm(k[15], (DEPTH, DH_A), 0.1),
        "lam_k2": nrm(k[16], (DEPTH, DH_A), 0.1),
        "g_subln": gain(k[17], (DEPTH, DV_A)),
        "g_q_b": gain(k[18], (DEPTH, DH_B)),
        "g_k_b": gain(k[19], (DEPTH, DH_B)),
        "g_sgu": gain(k[20], (DEPTH, C_WIDTH)),
        "w_spatial": nrm(k[21], (DEPTH, C_GROUPS, CHUNK, CHUNK), 0.5 * CHUNK ** -0.5),
        "b_spatial": 1.0 + nrm(k[22], (DEPTH, C_GROUPS, CHUNK), 0.01),
        "w_branch_a": nrm(k[23], (DEPTH, H_A * DV_A, D_MODEL), (H_A * DV_A) ** -0.5),
        "w_branch_b": nrm(k[24], (DEPTH, H_B * DH_B, D_MODEL), (H_B * DH_B) ** -0.5),
        "w_branch_c": nrm(k[25], (DEPTH, C_WIDTH, D_MODEL), C_WIDTH ** -0.5),
        "w_gate": nrm(k[26], (DEPTH, D_MODEL, N_BRANCH * D_MODEL), D_MODEL ** -0.5),
        "b_gate": nrm(k[27], (DEPTH, N_BRANCH * D_MODEL), 0.01),
        "w_out": nrm(k[28], (DEPTH, D_MODEL, D_MODEL), D_MODEL ** -0.5),
        "g_ffn": gain(k[29], (DEPTH, D_MODEL)),
        "w_ffn_gate": nrm(k[30], (DEPTH, D_MODEL, D_FF), D_MODEL ** -0.5),
        "w_ffn_up": nrm(k[31], (DEPTH, D_MODEL, D_FF), D_MODEL ** -0.5),
        "w_ffn_down": nrm(k[32], (DEPTH, D_FF, D_MODEL), D_FF ** -0.5),
    }


def reference(x_prompt, x_sample, cache_diff_k, cache_diff_v, cache_fox_k, cache_fox_v,
              cache_fox_logf, page_table, g_mix, w_in, b_forget, g_q_a, g_k_a,
              lam_q1, lam_k1, lam_q2, lam_k2, g_subln, g_q_b, g_k_b, g_sgu,
              w_spatial, b_spatial, w_branch_a, w_branch_b, w_branch_c, w_gate, b_gate,
              w_out, g_ffn, w_ffn_gate, w_ffn_up, w_ffn_down):
    params = (g_mix, w_in, b_forget, g_q_a, g_k_a, lam_q1, lam_k1, lam_q2, lam_k2, g_subln,
              g_q_b, g_k_b, g_sgu, w_spatial, b_spatial, w_branch_a, w_branch_b, w_branch_c,
              w_gate, b_gate, w_out, g_ffn, w_ffn_gate, w_ffn_up, w_ffn_down)
    seq_len = x_prompt.shape[1]
    dec_len = x_sample.shape[1]
    past_len = page_table.shape[1] * PAGE_SIZE
    pos_prompt = jnp.arange(seq_len, dtype=jnp.int32)
    qpos_sample = past_len + jnp.arange(dec_len, dtype=jnp.int32)
    kpos_sample = jnp.arange(past_len + dec_len, dtype=jnp.int32)

    y_p, y_s = x_prompt, x_sample
    rows_p, rows_s = [], []
    for layer in range(DEPTH):
        lp = tuple(p[layer] for p in params)
        lam_init = 0.8 - 0.6 * math.exp(-0.3 * layer)
        y_p, r_p = trunk_layer(y_p, pos_prompt, pos_prompt, None, lp, lam_init)
        past = (gather_pages(cache_diff_k[layer], page_table),
                gather_pages(cache_diff_v[layer], page_table),
                gather_pages(cache_fox_k[layer], page_table),
                gather_pages(cache_fox_v[layer], page_table),
                gather_pages(cache_fox_logf[layer], page_table))
        y_s, r_s = trunk_layer(y_s, qpos_sample, kpos_sample, past, lp, lam_init)
        rows_p.append(r_p)
        rows_s.append(r_s)

    stack = lambda rows, i: jnp.stack([r[i] for r in rows], axis=0)
    return (y_p, y_s,
            stack(rows_p, 0), stack(rows_p, 1), stack(rows_p, 2), stack(rows_p, 3), stack(rows_p, 4),
            stack(rows_s, 0), stack(rows_s, 1), stack(rows_s, 2), stack(rows_s, 3), stack(rows_s, 4),
            stack(rows_s, 5))
```

```python
import functools
import math

import numpy as np
import jax
import jax.numpy as jnp
from jax import lax
from jax.experimental import pallas as pl
from jax.experimental.pallas import tpu as pltpu

F32 = jnp.float32
BF16 = jnp.bfloat16

N_HEADS = 8
HEAD_W = 128
MAP_W = 64
SEC_W = N_HEADS * HEAD_W
N_SEC = 8
CHUNK = 128
PAGE = 128
PAGE_FLAT = PAGE * N_HEADS
AUG_W = 128
EPS = 1e-6
NEG_INF = float("-inf")

ROW_TILE = 256
VMEM_LIMIT = 56 * 1024 * 1024


def _params(*sem):
    return pltpu.CompilerParams(dimension_semantics=sem, vmem_limit_bytes=VMEM_LIMIT)


def _pick_tile(n, candidates):
    for c in candidates:
        if n % c == 0:
            return c
    raise ValueError(f"no tile in {candidates} divides {n}")


def _rms(x, g):
    ms = jnp.mean(x * x, axis=-1, keepdims=True)
    return x * lax.rsqrt(ms + EPS) * g


def _split3(x):
    hi = x.astype(BF16).astype(F32)
    r = x - hi
    mid = r.astype(BF16).astype(F32)
    lo = (r - mid).astype(BF16).astype(F32)
    return hi, mid, lo


def _gelu(x):
    return 0.5 * x * (1.0 + lax.erf(x * (2.0 ** -0.5)))


def _dot(a, b):
    return jnp.dot(a, b, preferred_element_type=F32)


def _dot_nt(a, b):
    return lax.dot_general(a, b, (((1,), (1,)), ((), ())), preferred_element_type=F32)


def _proj_in_kernel(x_ref, g_ref, w_ref, wf_ref, p_ref, pf_ref, h_ref):
    @pl.when(pl.program_id(1) == 0)
    def _():
        hb = _rms(x_ref[...], g_ref[...]).astype(BF16)
        h_ref[...] = hb
        pf_ref[...] = _dot(hb, wf_ref[...])

    p_ref[...] = _dot(h_ref[...], w_ref[...])


def _proj_in(x, g, w_main, w_f):
    m, d = x.shape
    n = w_main.shape[1]
    tm = _pick_tile(m, (768, 512, 256))
    tn = _pick_tile(n, (1024, 512))
    return pl.pallas_call(
        _proj_in_kernel,
        grid=(m // tm, n // tn),
        in_specs=[
            pl.BlockSpec((tm, d), lambda i, j: (i, 0)),
            pl.BlockSpec((1, d), lambda i, j: (0, 0)),
            pl.BlockSpec((d, tn), lambda i, j: (0, j)),
            pl.BlockSpec((d, AUG_W), lambda i, j: (0, 0)),
        ],
        out_specs=[
            pl.BlockSpec((tm, tn), lambda i, j: (i, j)),
            pl.BlockSpec((tm, AUG_W), lambda i, j: (i, 0)),
            pl.BlockSpec((tm, d), lambda i, j: (i, 0)),
        ],
        out_shape=[
            jax.ShapeDtypeStruct((m, n), F32),
            jax.ShapeDtypeStruct((m, AUG_W), F32),
            jax.ShapeDtypeStruct((m, d), BF16),
        ],
        compiler_params=_params("parallel", "arbitrary"),
        name="proj_in",
    )(x, g, w_main, w_f)


def _norm_halves(xh, g):
    lo = lax.broadcasted_iota(jnp.int32, xh.shape, 1) < MAP_W
    sq = xh * xh
    s_lo = jnp.sum(jnp.where(lo, sq, 0.0), axis=-1, keepdims=True)
    s_hi = jnp.sum(jnp.where(lo, 0.0, sq), axis=-1, keepdims=True)
    ms = jnp.where(lo, s_lo, s_hi) * (1.0 / MAP_W)
    return xh * lax.rsqrt(ms + EPS) * g


def _post_kernel(qa_ref, ka_ref, va_ref, qb_ref, kb_ref, vb_ref, uc_ref, vc_ref, pf_ref,
                 gqa_ref, gka_ref, gqb_ref, gkb_ref, gsgu_ref, bf_ref, slope_ref,
                 wm_ref, bm_ref, ltri_ref,
                 kan_ref, kbn_ref, logf_ref, vcn_ref, oc_ref,
                 qa_o, ka_o, va_o, qb_o, kb_o, vb_o,
                 carry_ref):
    i = pl.program_id(0)
    t = qa_ref.shape[0]
    lane = lax.broadcasted_iota(jnp.int32, (t, HEAD_W), 1)
    row = lax.broadcasted_iota(jnp.int32, (t, HEAD_W), 0) + i * t
    lo_half = lane < MAP_W

    @pl.when(i == 0)
    def _():
        carry_ref[...] = jnp.zeros_like(carry_ref)

    logf = jnp.where(lane < N_HEADS, jax.nn.log_sigmoid(pf_ref[...] + bf_ref[...]), 0.0)
    logf_ref[...] = logf
    ltri = ltri_ref[...]
    hi, mid, lo = _split3(logf)
    cum = (_dot(ltri, hi.astype(BF16)) + _dot(ltri, mid.astype(BF16)) + _dot(ltri, lo.astype(BF16))
           + carry_ref[...])
    carry_ref[...] = cum[t - 1:t, :]

    gl = _gelu(vc_ref[...])
    vcn = _rms(gl, gsgu_ref[...])
    vcn_ref[...] = vcn
    bm = bm_ref[...]

    ones_q2 = jnp.where(lane < 2, 1.0, 0.0)
    ones_q3 = jnp.where(lane < 3, 1.0, 0.0)
    row_hi = (row >> 7).astype(F32) * float(PAGE)
    row_lo = (row & (PAGE - 1)).astype(F32)

    for h in range(N_HEADS):
        cs = slice(h * HEAD_W, (h + 1) * HEAD_W)
        mixed = _dot(wm_ref[h], vcn[:, cs].astype(BF16)) + bm[:, h:h + 1]
        oc_ref[:, cs] = (_gelu(uc_ref[:, cs]) * mixed).astype(BF16)

        qn = _norm_halves(qa_ref[:, cs], gqa_ref[...]) * (MAP_W ** -0.5)
        kn = _norm_halves(ka_ref[:, cs], gka_ref[...])
        kan_ref[:, cs] = kn
        qa_o[h, 0] = jnp.concatenate([jnp.where(lo_half, qn, 0.0), ones_q2], axis=-1).astype(BF16)
        qa_o[h, 1] = jnp.concatenate([jnp.where(lo_half, 0.0, qn), ones_q2], axis=-1).astype(BF16)
        slope = slope_ref[:, h:h + 1]
        aug = jnp.where(lane == 0, row_hi * slope, jnp.where(lane == 1, row_lo * slope, 0.0))
        ka_o[h] = jnp.concatenate([kn, aug], axis=-1).astype(BF16)
        va_o[h] = va_ref[:, cs].astype(BF16)

        qn = _rms(qb_ref[:, cs], gqb_ref[...]) * (HEAD_W ** -0.5)
        kn = _rms(kb_ref[:, cs], gkb_ref[...])
        kbn_ref[:, cs] = kn
        qb_o[h, 0] = jnp.concatenate([qn, ones_q3], axis=-1).astype(BF16)
        c_hi, c_mid, c_lo = _split3(-cum[:, h:h + 1])
        aug = jnp.where(lane == 0, c_hi, jnp.where(lane == 1, c_mid, jnp.where(lane == 2, c_lo, 0.0)))
        kb_o[h] = jnp.concatenate([kn, aug], axis=-1).astype(BF16)
        vb_o[h] = vb_ref[:, cs].astype(BF16)


def _post(p, pf, gqa, gka, gqb, gkb, gsgu, bf, slope, wm, bm, ltri, n_prompt_tiles):
    m = p.shape[0]
    t = ROW_TILE
    nt = m // t

    def sec(k):
        return pl.BlockSpec((t, SEC_W), lambda i, k=k: (i, k))

    def whole(a):
        nd = a.ndim
        return pl.BlockSpec(a.shape, lambda i: (0,) * nd)

    def sel(i):
        return jnp.where(i < n_prompt_tiles, 0, 1)

    row_f32 = lambda w: pl.BlockSpec((t, w), lambda i: (i, 0))
    out_shape = [
        jax.ShapeDtypeStruct((m, SEC_W), F32),
        jax.ShapeDtypeStruct((m, SEC_W), F32),
        jax.ShapeDtypeStruct((m, HEAD_W), F32),
        jax.ShapeDtypeStruct((m, SEC_W), F32),
        jax.ShapeDtypeStruct((m, SEC_W), BF16),
        jax.ShapeDtypeStruct((N_HEADS, 2, m, HEAD_W + AUG_W), BF16),
        jax.ShapeDtypeStruct((N_HEADS, m, HEAD_W + AUG_W), BF16),
        jax.ShapeDtypeStruct((N_HEADS, m, HEAD_W), BF16),
        jax.ShapeDtypeStruct((N_HEADS, 1, m, HEAD_W + AUG_W), BF16),
        jax.ShapeDtypeStruct((N_HEADS, m, HEAD_W + AUG_W), BF16),
        jax.ShapeDtypeStruct((N_HEADS, m, HEAD_W), BF16),
    ]
    out_specs = [
        row_f32(SEC_W), row_f32(SEC_W), row_f32(HEAD_W), row_f32(SEC_W), row_f32(SEC_W),
        pl.BlockSpec((N_HEADS, 2, t, HEAD_W + AUG_W), lambda i: (0, 0, i, 0)),
        pl.BlockSpec((N_HEADS, t, HEAD_W + AUG_W), lambda i: (0, i, 0)),
        pl.BlockSpec((N_HEADS, t, HEAD_W), lambda i: (0, i, 0)),
        pl.BlockSpec((N_HEADS, 1, t, HEAD_W + AUG_W), lambda i: (0, 0, i, 0)),
        pl.BlockSpec((N_HEADS, t, HEAD_W + AUG_W), lambda i: (0, i, 0)),
        pl.BlockSpec((N_HEADS, t, HEAD_W), lambda i: (0, i, 0)),
    ]
    in_specs = [sec(k) for k in range(N_SEC)] + [
        row_f32(AUG_W),
        whole(gqa), whole(gka), whole(gqb), whole(gkb), whole(gsgu), whole(bf), whole(slope),
        pl.BlockSpec((None, N_HEADS, t, t), lambda i: (sel(i), 0, 0, 0)),
        pl.BlockSpec((None, t, HEAD_W), lambda i: (sel(i), 0, 0)),
        whole(ltri),
    ]
    return pl.pallas_call(
        _post_kernel,
        grid=(nt,),
        in_specs=in_specs,
        out_specs=out_specs,
        out_shape=out_shape,
        scratch_shapes=[pltpu.VMEM((1, HEAD_W), F32)],
        compiler_params=_params("arbitrary"),
        name="post",
    )(*([p] * N_SEC), pf, gqa, gka, gqb, gkb, gsgu, bf, slope, wm, bm, ltri)


def _lam_value(lam_ref, lam_init):
    a = lam_ref[...]
    s1 = jnp.sum(a[0:1] * a[1:2], axis=-1, keepdims=True)
    s2 = jnp.sum(a[2:3] * a[3:4], axis=-1, keepdims=True)
    return jnp.exp(s1) - jnp.exp(s2) + lam_init


def _finish_heads(acc, l, n_maps, lam_ref, gsub_ref, lam_init):
    if n_maps == 1:
        return acc / l
    r = acc.shape[0] // 2
    o = acc[:r] / l[:r] - _lam_value(lam_ref, lam_init) * (acc[r:] / l[r:])
    return _rms(o, gsub_ref[...]) * (1.0 - lam_init)


def _flash_kernel(qi_tab, ki_tab, q_ref, k_ref, v_ref, lam_ref, gsub_ref, o_ref,
                  m_scr, l_scr, acc_scr, *, n_maps, lam_init):
    s = pl.program_id(1)
    qi = qi_tab[s]
    ki = ki_tab[s]
    tq = q_ref.shape[1]
    tk = k_ref.shape[0]

    @pl.when(ki == 0)
    def _():
        m_scr[...] = jnp.full_like(m_scr, NEG_INF)
        l_scr[...] = jnp.zeros_like(l_scr)
        acc_scr[...] = jnp.zeros_like(acc_scr)

    def step(diagonal):
        k = k_ref[...]
        v = v_ref[...]
        for mp in range(n_maps):
            sc = _dot_nt(q_ref[mp], k)
            if diagonal:
                r = lax.broadcasted_iota(jnp.int32, (tq, tk), 0)
                c = lax.broadcasted_iota(jnp.int32, (tq, tk), 1)
                sc = jnp.where(r >= c, sc, NEG_INF)
            m_prev = m_scr[mp]
            m_new = jnp.maximum(m_prev, jnp.max(sc, axis=-1, keepdims=True))
            alpha = jnp.exp(m_prev - m_new)
            p = jnp.exp(sc - m_new)
            l_scr[mp] = alpha * l_scr[mp] + jnp.sum(p, axis=-1, keepdims=True)
            acc_scr[mp] = alpha * acc_scr[mp] + _dot(p.astype(BF16), v)
            m_scr[mp] = m_new

    @pl.when(ki < qi)
    def _():
        step(False)

    @pl.when(ki == qi)
    def _():
        step(True)
        acc = acc_scr[...].reshape(n_maps * tq, HEAD_W)
        l = l_scr[...].reshape(n_maps * tq, 1)
        o_ref[...] = _finish_heads(acc, l, n_maps, lam_ref, gsub_ref, lam_init).astype(o_ref.dtype)


def _flash(q, k, v, lam_rows, gsub, n_rows, lam_init):
    n_maps = q.shape[1]
    tb = _pick_tile(n_rows, (512, 256, 128))
    nq = n_rows // tb
    qi_tab = np.concatenate([np.full(i + 1, i, np.int32) for i in range(nq)])
    ki_tab = np.concatenate([np.arange(i + 1, dtype=np.int32) for i in range(nq)])
    grid_spec = pltpu.PrefetchScalarGridSpec(
        num_scalar_prefetch=2,
        grid=(N_HEADS, len(qi_tab)),
        in_specs=[
            pl.BlockSpec((None, n_maps, tb, HEAD_W + AUG_W), lambda h, s, qt, kt: (h, 0, qt[s], 0)),
            pl.BlockSpec((None, tb, HEAD_W + AUG_W), lambda h, s, qt, kt: (h, kt[s], 0)),
            pl.BlockSpec((None, tb, HEAD_W), lambda h, s, qt, kt: (h, kt[s], 0)),
            pl.BlockSpec(lam_rows.shape, lambda h, s, qt, kt: (0, 0)),
            pl.BlockSpec(gsub.shape, lambda h, s, qt, kt: (0, 0)),
        ],
        out_specs=pl.BlockSpec((tb, HEAD_W), lambda h, s, qt, kt: (qt[s], h)),
        scratch_shapes=[
            pltpu.VMEM((n_maps, tb, 1), F32),
            pltpu.VMEM((n_maps, tb, 1), F32),
            pltpu.VMEM((n_maps, tb, HEAD_W), F32),
        ],
    )
    return pl.pallas_call(
        functools.partial(_flash_kernel, n_maps=n_maps, lam_init=lam_init),
        grid_spec=grid_spec,
        out_shape=jax.ShapeDtypeStruct((n_rows, SEC_W), BF16),
        compiler_params=_params("parallel", "arbitrary"),
        name=f"flash{n_maps}",
    )(jnp.asarray(qi_tab), jnp.asarray(ki_tab), q, k, v, lam_rows, gsub)


def _pagecum_kernel(x_ref, u_ref, ut_ref, wc_ref, wt_ref):
    hi, mid, lo = (a.astype(BF16) for a in _split3(x_ref[...]))
    u = u_ref[...]
    ut = ut_ref[...]
    wc_ref[...] = _dot(hi, u) + _dot(mid, u) + _dot(lo, u)
    wt_ref[...] = _dot(hi, ut) + _dot(mid, ut) + _dot(lo, ut)


def _pagecum(x, u, ut):
    r = x.shape[0]
    tr = _pick_tile(r, (512, 256, 128, 64, 32, 16, 8))
    spec = pl.BlockSpec((tr, PAGE_FLAT), lambda i: (i, 0))
    wspec = pl.BlockSpec((PAGE_FLAT, PAGE_FLAT), lambda i: (0, 0))
    return pl.pallas_call(
        _pagecum_kernel,
        grid=(r // tr,),
        in_specs=[spec, wspec, wspec],
        out_specs=[spec, spec],
        out_shape=[jax.ShapeDtypeStruct((r, PAGE_FLAT), F32)] * 2,
        compiler_params=_params("parallel"),
        name="pagecum",
    )(x, u, ut)


PAGES_PER_STEP = 4


def _decode_kernel(pt_ref, *refs, n_maps, forget, n_pages, lam_init):
    pps = PAGES_PER_STEP
    q_ref = refs[0]
    k_refs = refs[1:1 + pps]
    v_refs = refs[1 + pps:1 + 2 * pps]
    pos = 1 + 2 * pps
    if forget:
        wc_refs = refs[pos:pos + pps]
        wt_refs = refs[pos + pps:pos + 2 * pps]
        pos += 2 * pps
        cnew_ref = refs[pos]
        pos += 1
    else:
        slope_ref = refs[pos]
        pos += 1
    knew_ref, vnew_ref, lam_ref, gsub_ref, o_ref, m_scr, l_scr, acc_scr, carry_scr = refs[pos:]

    step = pl.program_id(1)
    n_steps = n_pages // pps
    q = q_ref[...]
    r = q.shape[0]

    @pl.when(step == 0)
    def _():
        m_scr[...] = jnp.full_like(m_scr, NEG_INF)
        l_scr[...] = jnp.zeros_like(l_scr)
        acc_scr[...] = jnp.zeros_like(acc_scr)
        carry_scr[...] = jnp.zeros_like(carry_scr)

    def attend(k, v, bias, extra_mask):
        c = k.shape[0]
        sc = _dot_nt(q, k.astype(BF16)) + bias
        row_head = (lax.broadcasted_iota(jnp.int32, (r, c), 0) >> 3) & (N_HEADS - 1)
        col = lax.broadcasted_iota(jnp.int32, (r, c), 1)
        ok = (col & (N_HEADS - 1)) == row_head
        if extra_mask is not None:
            ok = ok & extra_mask(col)
        sc = jnp.where(ok, sc, NEG_INF)
        m_prev = m_scr[...]
        m_new = jnp.maximum(m_prev, jnp.max(sc, axis=-1, keepdims=True))
        alpha = jnp.exp(m_prev - m_new)
        p = jnp.exp(sc - m_new)
        l_scr[...] = alpha * l_scr[...] + jnp.sum(p, axis=-1, keepdims=True)
        acc_scr[...] = alpha * acc_scr[...] + _dot(p.astype(BF16), v.astype(BF16))
        m_scr[...] = m_new

    @pl.when(step < n_steps)
    def _():
        for j in range(pps):
            if forget:
                bias = -(carry_scr[...] + wc_refs[j][...])
                carry_scr[...] = carry_scr[...] + wt_refs[j][...]
            else:
                page_pos = (step * pps + j) * PAGE
                tok = (lax.broadcasted_iota(jnp.int32, (1, PAGE_FLAT), 1) >> 3) + page_pos
                bias = slope_ref[...] * tok.astype(F32)
            attend(k_refs[j][...], v_refs[j][...], bias, None)

    @pl.when(step == n_steps)
    def _():
        c = knew_ref.shape[0]
        if forget:
            bias = -(carry_scr[:, :c] + cnew_ref[:, :c])
        else:
            tok = (lax.broadcasted_iota(jnp.int32, (1, c), 1) >> 3) + n_pages * PAGE
            bias = slope_ref[:, :c] * tok.astype(F32)
        q_idx = lax.broadcasted_iota(jnp.int32, (r, c), 0) & 7
        attend(knew_ref[...], vnew_ref[...], bias, lambda col: (col >> 3) <= q_idx)
        o_ref[...] = _finish_heads(acc_scr[...], l_scr[...], n_maps, lam_ref, gsub_ref, lam_init)


def _decode(page_table, q, k_cache, v_cache, layer, knew, vnew, lam_rows, gsub, lam_init,
            *, forget, wc=None, wt=None, cnew=None, slope_flat=None):
    b, r, _ = q.shape
    n_maps = r // (N_HEADS * 8)
    n_pages = page_table.shape[1]
    pps = PAGES_PER_STEP
    assert n_pages % pps == 0
    n_steps = n_pages // pps
    c_new = knew.shape[1]

    def page_idx(s, j):
        return jnp.minimum(s, n_steps - 1) * pps + j

    def cache_spec(j):
        return pl.BlockSpec((None, None, PAGE_FLAT, HEAD_W),
                            lambda bi, s, pt, j=j: (layer, pt[bi, page_idx(s, j)], 0, 0))

    def cum_spec(j):
        return pl.BlockSpec((None, 1, PAGE_FLAT), lambda bi, s, pt, j=j: (pt[bi, page_idx(s, j)], 0, 0))

    per_seq = lambda shape: pl.BlockSpec((None,) + shape, lambda bi, s, pt: (bi, 0, 0))
    whole2 = lambda a: pl.BlockSpec(a.shape, lambda bi, s, pt: (0, 0))

    in_specs = [per_seq((r, HEAD_W))] + [cache_spec(j) for j in range(pps)] * 1
    in_specs += [cache_spec(j) for j in range(pps)]
    args = [q] + [k_cache] * pps + [v_cache] * pps
    if forget:
        in_specs += [cum_spec(j) for j in range(pps)] + [cum_spec(j) for j in range(pps)]
        args += [wc] * pps + [wt] * pps
        in_specs += [per_seq((1, PAGE_FLAT))]
        args += [cnew]
    else:
        in_specs += [whole2(slope_flat)]
        args += [slope_flat]
    in_specs += [per_seq((c_new, HEAD_W)), per_seq((c_new, HEAD_W)), whole2(lam_rows), whole2(gsub)]
    args += [knew, vnew, lam_rows, gsub]

    grid_spec = pltpu.PrefetchScalarGridSpec(
        num_scalar_prefetch=1,
        grid=(b, n_steps + 1),
        in_specs=in_specs,
        out_specs=per_seq((r // n_maps, HEAD_W)),
        scratch_shapes=[
            pltpu.VMEM((r, 1), F32),
            pltpu.VMEM((r, 1), F32),
            pltpu.VMEM((r, HEAD_W), F32),
            pltpu.VMEM((1, PAGE_FLAT), F32),
        ],
    )
    return pl.pallas_call(
        functools.partial(_decode_kernel, n_maps=n_maps, forget=forget, n_pages=n_pages,
                          lam_init=lam_init),
        grid_spec=grid_spec,
        out_shape=jax.ShapeDtypeStruct((b, r // n_maps, HEAD_W), F32),
        compiler_params=_params("parallel", "arbitrary"),
        name="decode_forget" if forget else "decode_diff",
    )(page_table, *args)


def _zmerge_kernel(h_ref, oa_ref, ob_ref, oc_ref, wa_ref, wb_ref, wc_ref,
                   wga_ref, wgb_ref, wgc_ref, bga_ref, bgb_ref, bgc_ref, z_ref):
    h = h_ref[...]
    ga = jax.nn.sigmoid(_dot(h, wga_ref[...]) + bga_ref[...])
    gb = jax.nn.sigmoid(_dot(h, wgb_ref[...]) + bgb_ref[...])
    gc = jax.nn.sigmoid(_dot(h, wgc_ref[...]) + bgc_ref[...])
    z = (ga * _dot(oa_ref[...], wa_ref[...]) + gb * _dot(ob_ref[...], wb_ref[...])
         + gc * _dot(oc_ref[...], wc_ref[...]))
    z_ref[...] = z.astype(z_ref.dtype)


def _zmerge(h, oa, ob, oc, wa, wb, wc, wg, bg):
    m, d = h.shape
    tm = _pick_tile(m, (768, 512, 256))
    tn = _pick_tile(d, (512, 256))
    nn = d // tn
    rows = lambda w: pl.BlockSpec((tm, w), lambda i, j: (i, 0))
    wcol = lambda kdim, off: pl.BlockSpec((kdim, tn), lambda i, j, off=off: (0, j + off))
    bcol = lambda off: pl.BlockSpec((1, tn), lambda i, j, off=off: (0, j + off))
    return pl.pallas_call(
        _zmerge_kernel,
        grid=(m // tm, nn),
        in_specs=[rows(d), rows(SEC_W), rows(SEC_W), rows(SEC_W),
                  wcol(SEC_W, 0), wcol(SEC_W, 0), wcol(SEC_W, 0),
                  wcol(d, 0), wcol(d, nn), wcol(d, 2 * nn),
                  bcol(0), bcol(nn), bcol(2 * nn)],
        out_specs=pl.BlockSpec((tm, tn), lambda i, j: (i, j)),
        out_shape=jax.ShapeDtypeStruct((m, d), BF16),
        compiler_params=_params("parallel", "arbitrary"),
        name="zmerge",
    )(h, oa, ob, oc, wa, wb, wc, wg, wg, wg, bg, bg, bg)


def _outproj_kernel(x_ref, z_ref, w_ref, o_ref):
    o_ref[...] = x_ref[...] + _dot(z_ref[...], w_ref[...])


def _outproj(x, z, w):
    m, d = x.shape
    tm = _pick_tile(m, (768, 512, 256))
    tn = _pick_tile(d, (1024, 512))
    return pl.pallas_call(
        _outproj_kernel,
        grid=(m // tm, d // tn),
        in_specs=[pl.BlockSpec((tm, tn), lambda i, j: (i, j)),
                  pl.BlockSpec((tm, d), lambda i, j: (i, 0)),
                  pl.BlockSpec((d, tn), lambda i, j: (0, j))],
        out_specs=pl.BlockSpec((tm, tn), lambda i, j: (i, j)),
        out_shape=jax.ShapeDtypeStruct((m, d), F32),
        compiler_params=_params("parallel", "arbitrary"),
        name="outproj",
    )(x, z, w)


def _ffn_kernel(x_ref, g_ref, wg_ref, wu_ref, wd_ref, o_ref, h_scr):
    @pl.when(pl.program_id(1) == 0)
    def _():
        x = x_ref[...]
        h_scr[...] = _rms(x, g_ref[...]).astype(BF16)
        o_ref[...] = x

    h = h_scr[...]
    a = jax.nn.silu(_dot(h, wg_ref[...])) * _dot(h, wu_ref[...])
    o_ref[...] += _dot(a.astype(BF16), wd_ref[...])


def _ffn(x, g, wg, wu, wd):
    m, d = x.shape
    f = wg.shape[1]
    tm = _pick_tile(m, (768, 512, 256))
    tf = _pick_tile(f, (512, 256))
    return pl.pallas_call(
        _ffn_kernel,
        grid=(m // tm, f // tf),
        in_specs=[pl.BlockSpec((tm, d), lambda i, j: (i, 0)),
                  pl.BlockSpec((1, d), lambda i, j: (0, 0)),
                  pl.BlockSpec((d, tf), lambda i, j: (0, j)),
                  pl.BlockSpec((d, tf), lambda i, j: (0, j)),
                  pl.BlockSpec((tf, d), lambda i, j: (j, 0))],
        out_specs=pl.BlockSpec((tm, d), lambda i, j: (i, 0)),
        out_shape=jax.ShapeDtypeStruct((m, d), F32),
        scratch_shapes=[pltpu.VMEM((tm, d), BF16)],
        compiler_params=_params("parallel", "arbitrary"),
        name="ffn",
    )(x, g, wg, wu, wd)


def _page_cum_matrices():
    idx = np.arange(PAGE_FLAT)
    tok, head = idx >> 3, idx & 7
    same = head[:, None] == head[None, :]
    u = same & (tok[:, None] <= tok[None, :])
    return jnp.asarray(u, BF16), jnp.asarray(same, BF16)


def _mix_matrices(w_spatial, b_spatial, n_sample_tok):
    t = ROW_TILE
    tril = jnp.where(jnp.tril(jnp.ones((CHUNK, CHUNK), bool)), w_spatial, 0.0)
    eye_p = jnp.eye(t // CHUNK, dtype=F32)
    wm_p = jnp.einsum("ab,gts->gatbs", eye_p, tril).reshape(N_HEADS, t, t)
    bm_p = jnp.tile(b_spatial, (1, t // CHUNK))
    n = n_sample_tok
    eye_s = jnp.eye(t // n, dtype=F32)
    wm_s = jnp.einsum("ab,gts->gatbs", eye_s, tril[:, :n, :n]).reshape(N_HEADS, t, t)
    bm_s = jnp.tile(b_spatial[:, :n], (1, t // n))
    wm = jnp.stack([wm_p, wm_s]).astype(BF16)
    bm = jnp.stack([bm_p, bm_s])
    bm = jnp.pad(jnp.swapaxes(bm, 1, 2), ((0, 0), (0, 0), (0, HEAD_W - N_HEADS)))
    return wm, bm


def _lane_row(v, width=HEAD_W):
    v = v.reshape(1, -1)
    return jnp.pad(v, ((0, 0), (0, width - v.shape[1])))


def kernel(x_prompt, x_sample, cache_diff_k, cache_diff_v, cache_fox_k, cache_fox_v, cache_fox_logf,
           page_table, g_mix, w_in, b_forget, g_q_a, g_k_a, lam_q1, lam_k1, lam_q2, lam_k2, g_subln,
           g_q_b, g_k_b, g_sgu, w_spatial, b_spatial, w_branch_a, w_branch_b, w_branch_c, w_gate,
           b_gate, w_out, g_ffn, w_ffn_gate, w_ffn_up, w_ffn_down):
    depth = g_mix.shape[0]
    n_batch, seq, d_model = x_prompt.shape
    dec_b, dec_t, _ = x_sample.shape
    n_phys = cache_diff_k.shape[1]
    n_dec = dec_b * dec_t
    assert n_batch == 1 and d_model == N_HEADS * 2 * HEAD_W
    assert seq % ROW_TILE == 0 and n_dec == ROW_TILE and dec_t == 8
    assert cache_diff_k.shape[2] == PAGE
    m = seq + n_dec

    x = jnp.concatenate([x_prompt.reshape(seq, d_model), x_sample.reshape(n_dec, d_model)], axis=0)

    cdk = cache_diff_k.reshape(depth, n_phys, PAGE_FLAT, HEAD_W)
    cdv = cache_diff_v.reshape(depth, n_phys, PAGE_FLAT, HEAD_W)
    cfk = cache_fox_k.reshape(depth, n_phys, PAGE_FLAT, HEAD_W)
    cfv = cache_fox_v.reshape(depth, n_phys, PAGE_FLAT, HEAD_W)
    cfl = cache_fox_logf.reshape(depth, n_phys, PAGE_FLAT)

    slopes = np.exp2(-8.0 * np.arange(1, N_HEADS + 1) / N_HEADS).astype(np.float32)
    slope_row = jnp.asarray(np.pad(slopes, (0, HEAD_W - N_HEADS)).reshape(1, HEAD_W))
    slope_flat = jnp.asarray(np.tile(slopes, PAGE).reshape(1, PAGE_FLAT))
    ltri = jnp.asarray(np.tril(np.ones((ROW_TILE, ROW_TILE), np.float32)), BF16)
    u_cum, u_tot = _page_cum_matrices()
    widths = np.cumsum([SEC_W] * 6 + [N_HEADS])

    rows_prompt, rows_sample = [], []
    for layer in range(depth):
        lam_init = 0.8 - 0.6 * math.exp(-0.3 * layer)
        w_l = w_in[layer]
        w_main = jnp.concatenate([w_l[:, :widths[5]], w_l[:, widths[6]:]], axis=1).astype(BF16)
        w_f = jnp.pad(w_l[:, widths[5]:widths[6]], ((0, 0), (0, AUG_W - N_HEADS))).astype(BF16)
        p, pf, h = _proj_in(x, g_mix[layer].reshape(1, d_model), w_main, w_f)

        wm, bm = _mix_matrices(w_spatial[layer], b_spatial[layer], dec_t)
        lam_rows = jnp.pad(jnp.stack([lam_q1[layer], lam_k1[layer], lam_q2[layer], lam_k2[layer]]),
                           ((0, 4), (0, HEAD_W - MAP_W)))
        gsub = g_subln[layer].reshape(1, HEAD_W)
        (kan, kbn, logf, vcn, oc, qa_o, ka_o, va_o, qb_o, kb_o, vb_o) = _post(
            p, pf,
            jnp.tile(g_q_a[layer], 2).reshape(1, HEAD_W), jnp.tile(g_k_a[layer], 2).reshape(1, HEAD_W),
            g_q_b[layer].reshape(1, HEAD_W), g_k_b[layer].reshape(1, HEAD_W),
            g_sgu[layer].reshape(1, SEC_W), _lane_row(b_forget[layer]), slope_row,
            wm, bm, ltri, seq // ROW_TILE)
        va = p[:, 2 * SEC_W:3 * SEC_W]
        vb = p[:, 5 * SEC_W:6 * SEC_W]

        oa_p = _flash(qa_o, ka_o, va_o, lam_rows, gsub, seq, lam_init)
        ob_p = _flash(qb_o, kb_o, vb_o, lam_rows, gsub, seq, lam_init)

        to_rows = lambda a: a[seq:].reshape(dec_b, dec_t * N_HEADS, HEAD_W)
        q_a = jnp.transpose(qa_o[:, :, seq:, :HEAD_W].reshape(N_HEADS, 2, dec_b, dec_t, HEAD_W),
                            (2, 1, 0, 3, 4)).reshape(dec_b, 2 * N_HEADS * dec_t, HEAD_W)
        q_b = jnp.transpose(qb_o[:, 0, seq:, :HEAD_W].reshape(N_HEADS, dec_b, dec_t, HEAD_W),
                            (1, 0, 2, 3)).reshape(dec_b, N_HEADS * dec_t, HEAD_W)
        oa_s = _decode(page_table, q_a, cdk, cdv, layer, to_rows(kan), to_rows(va), lam_rows, gsub,
                       lam_init, forget=False, slope_flat=slope_flat)
        wc, wt = _pagecum(cfl[layer], u_cum, u_tot)
        lf_new = jnp.pad(logf[seq:, :N_HEADS].reshape(dec_b, dec_t * N_HEADS),
                         ((0, 0), (0, PAGE_FLAT - dec_t * N_HEADS)))
        cnew, _ = _pagecum(lf_new, u_cum, u_tot)
        ob_s = _decode(page_table, q_b, cfk, cfv, layer, to_rows(kbn), to_rows(vb), lam_rows, gsub,
                       lam_init, forget=True, wc=wc.reshape(n_phys, 1, PAGE_FLAT),
                       wt=wt.reshape(n_phys, 1, PAGE_FLAT), cnew=cnew.reshape(dec_b, 1, PAGE_FLAT))
        from_rows = lambda o: jnp.transpose(o.reshape(dec_b, N_HEADS, dec_t, HEAD_W),
                                            (0, 2, 1, 3)).reshape(n_dec, SEC_W).astype(BF16)
        oa = jnp.concatenate([oa_p, from_rows(oa_s)], axis=0)
        ob = jnp.concatenate([ob_p, from_rows(ob_s)], axis=0)

        z = _zmerge(h, oa, ob, oc, w_branch_a[layer].astype(BF16), w_branch_b[layer].astype(BF16),
                    w_branch_c[layer].astype(BF16), w_gate[layer].astype(BF16),
                    b_gate[layer].reshape(1, -1))
        x = _outproj(x, z, w_out[layer].astype(BF16))
        x = _ffn(x, g_ffn[layer].reshape(1, d_model), w_ffn_gate[layer].astype(BF16),
                 w_ffn_up[layer].astype(BF16), w_ffn_down[layer].astype(BF16))

        lf = logf[:, :N_HEADS]
        rows_prompt.append((kan[:seq], va[:seq], kbn[:seq], vb[:seq], lf[:seq]))
        rows_sample.append((kan[seq:], va[seq:], kbn[seq:], vb[seq:], lf[seq:], vcn[seq:]))

    stack = lambda rows, i, shape: jnp.stack([r[i] for r in rows], axis=0).reshape((depth,) + shape)
    p_shape, s_shape = (n_batch, seq), (dec_b, dec_t)
    return (
        x[:seq].reshape(n_batch, seq, d_model),
        x[seq:].reshape(dec_b, dec_t, d_model),
        stack(rows_prompt, 0, p_shape + (N_HEADS, 2, MAP_W)),
        stack(rows_prompt, 1, p_shape + (N_HEADS, HEAD_W)),
        stack(rows_prompt, 2, p_shape + (N_HEADS, HEAD_W)),
        stack(rows_prompt, 3, p_shape + (N_HEADS, HEAD_W)),
        stack(rows_prompt, 4, p_shape + (N_HEADS,)),
        stack(rows_sample, 0, s_shape + (N_HEADS, 2, MAP_W)),
        stack(rows_sample, 1, s_shape + (N_HEADS, HEAD_W)),
        stack(rows_sample, 2, s_shape + (N_HEADS, HEAD_W)),
        stack(rows_sample, 3, s_shape + (N_HEADS, HEAD_W)),
        stack(rows_sample, 4, s_shape + (N_HEADS,)),
        stack(rows_sample, 5, s_shape + (SEC_W,)),
    )
```

```python
import functools
import math

import numpy as np
import jax
import jax.numpy as jnp
from jax import lax
from jax.experimental import pallas as pl
from jax.experimental.pallas import tpu as pltpu

F32 = jnp.float32
BF16 = jnp.bfloat16

N_HEADS = 8
HEAD_W = 128
MAP_W = 64
SEC_W = N_HEADS * HEAD_W
N_SEC = 8
CHUNK = 128
PAGE = 128
PAGE_FLAT = PAGE * N_HEADS
AUG_W = 128
ONES_ROWS = 16
LOG2E = 1.4426950408889634
EPS = 1e-6
NEG_INF = float("-inf")

ROW_TILE = 256
VMEM_LIMIT = 56 * 1024 * 1024


def _params(*sem):
    return pltpu.CompilerParams(dimension_semantics=sem, vmem_limit_bytes=VMEM_LIMIT)


def _pick_tile(n, candidates):
    for c in candidates:
        if n % c == 0:
            return c
    raise ValueError(f"no tile in {candidates} divides {n}")


def _rms(x, g):
    ms = jnp.mean(x * x, axis=-1, keepdims=True)
    return x * lax.rsqrt(ms + EPS) * g


def _split3(x):
    hi = x.astype(BF16).astype(F32)
    r = x - hi
    mid = r.astype(BF16).astype(F32)
    lo = (r - mid).astype(BF16).astype(F32)
    return hi, mid, lo


def _gelu(x):
    return 0.5 * x * (1.0 + lax.erf(x * (2.0 ** -0.5)))


def _dot(a, b):
    return jnp.dot(a, b, preferred_element_type=F32)


def _dot_nt(a, b):
    return lax.dot_general(a, b, (((1,), (1,)), ((), ())), preferred_element_type=F32)


def _proj_in_kernel(x_ref, g_ref, w_ref, wf_ref, p_ref, pf_ref, h_ref):
    @pl.when(pl.program_id(1) == 0)
    def _():
        hb = _rms(x_ref[...], g_ref[...]).astype(BF16)
        h_ref[...] = hb
        pf_ref[...] = _dot(hb, wf_ref[...])

    p_ref[...] = _dot(h_ref[...], w_ref[...])


def _proj_in(x, g, w_main, w_f):
    m, d = x.shape
    n = w_main.shape[1]
    tm = _pick_tile(m, (768, 512, 256))
    tn = _pick_tile(n, (1024, 512))
    return pl.pallas_call(
        _proj_in_kernel,
        grid=(m // tm, n // tn),
        in_specs=[
            pl.BlockSpec((tm, d), lambda i, j: (i, 0)),
            pl.BlockSpec((1, d), lambda i, j: (0, 0)),
            pl.BlockSpec((d, tn), lambda i, j: (0, j)),
            pl.BlockSpec((d, AUG_W), lambda i, j: (0, 0)),
        ],
        out_specs=[
            pl.BlockSpec((tm, tn), lambda i, j: (i, j)),
            pl.BlockSpec((tm, AUG_W), lambda i, j: (i, 0)),
            pl.BlockSpec((tm, d), lambda i, j: (i, 0)),
        ],
        out_shape=[
            jax.ShapeDtypeStruct((m, n), F32),
            jax.ShapeDtypeStruct((m, AUG_W), F32),
            jax.ShapeDtypeStruct((m, d), BF16),
        ],
        compiler_params=_params("parallel", "arbitrary"),
        name="proj_in",
    )(x, g, w_main, w_f)


def _norm_halves(xh, g):
    lo = lax.broadcasted_iota(jnp.int32, xh.shape, 1) < MAP_W
    sq = xh * xh
    s_lo = jnp.sum(jnp.where(lo, sq, 0.0), axis=-1, keepdims=True)
    s_hi = jnp.sum(jnp.where(lo, 0.0, sq), axis=-1, keepdims=True)
    ms = jnp.where(lo, s_lo, s_hi) * (1.0 / MAP_W)
    return xh * lax.rsqrt(ms + EPS) * g


def _post_kernel(qa_ref, ka_ref, va_ref, qb_ref, kb_ref, vb_ref, uc_ref, vc_ref, pf_ref,
                 gqa_ref, gka_ref, gqb_ref, gkb_ref, gsgu_ref, bf_ref, slope_ref,
                 wm_ref, bm_ref, ltri_ref,
                 kan_ref, kbn_ref, logf_ref, vcn_ref, oc_ref,
                 qa_o, ka_o, va_o, qb_o, kb_o, vb_o,
                 carry_ref):
    i = pl.program_id(0)
    t = qa_ref.shape[0]
    lane = lax.broadcasted_iota(jnp.int32, (t, HEAD_W), 1)
    row = lax.broadcasted_iota(jnp.int32, (t, HEAD_W), 0) + i * t

    @pl.when(i == 0)
    def _():
        carry_ref[...] = jnp.zeros_like(carry_ref)

    logf = jnp.where(lane < N_HEADS, jax.nn.log_sigmoid(pf_ref[...] + bf_ref[...]), 0.0)
    logf_ref[...] = logf
    ltri = ltri_ref[...]
    hi, mid, lo = _split3(logf)
    cum = (_dot(ltri, hi.astype(BF16)) + _dot(ltri, mid.astype(BF16)) + _dot(ltri, lo.astype(BF16))
           + carry_ref[...])
    carry_ref[...] = cum[t - 1:t, :]

    gl = _gelu(vc_ref[...])
    vcn = _rms(gl, gsgu_ref[...])
    vcn_ref[...] = vcn
    bm = bm_ref[...]

    feat = lax.broadcasted_iota(jnp.int32, (HEAD_W, t), 0)
    lo_feat = feat < MAP_W
    ones_q = jnp.where(feat < 3, 1.0, 0.0)
    ones_v = jnp.ones((ONES_ROWS, t), F32)
    pos = row.astype(F32)

    def key_term(x):
        hi, mid, lo = _split3(x)
        return jnp.where(lane == 0, hi, jnp.where(lane == 1, mid, jnp.where(lane == 2, lo, 0.0)))

    for h in range(N_HEADS):
        cs = slice(h * HEAD_W, (h + 1) * HEAD_W)
        mixed = _dot(wm_ref[h], vcn[:, cs].astype(BF16)) + bm[:, h:h + 1]
        oc_ref[:, cs] = (_gelu(uc_ref[:, cs]) * mixed).astype(BF16)

        qt = (_norm_halves(qa_ref[:, cs], gqa_ref[...]) * (MAP_W ** -0.5 * LOG2E)).T
        kn = _norm_halves(ka_ref[:, cs], gka_ref[...])
        kan_ref[:, cs] = kn
        qa_o[h, 0] = jnp.concatenate([jnp.where(lo_feat, qt, 0.0), ones_q], axis=0).astype(BF16)
        qa_o[h, 1] = jnp.concatenate([jnp.where(lo_feat, 0.0, qt), ones_q], axis=0).astype(BF16)
        aug = key_term(pos[:, :1] * (slope_ref[:, h:h + 1] * LOG2E))
        ka_o[h] = jnp.concatenate([kn, aug], axis=-1).astype(BF16)
        va_o[h] = jnp.concatenate([va_ref[:, cs].T, ones_v], axis=0).astype(BF16)

        qt = (_rms(qb_ref[:, cs], gqb_ref[...]) * (HEAD_W ** -0.5 * LOG2E)).T
        kn = _rms(kb_ref[:, cs], gkb_ref[...])
        kbn_ref[:, cs] = kn
        qb_o[h, 0] = jnp.concatenate([qt, ones_q], axis=0).astype(BF16)
        aug = key_term(cum[:, h:h + 1] * (-LOG2E))
        kb_o[h] = jnp.concatenate([kn, aug], axis=-1).astype(BF16)
        vb_o[h] = jnp.concatenate([vb_ref[:, cs].T, ones_v], axis=0).astype(BF16)


def _post(p, pf, gqa, gka, gqb, gkb, gsgu, bf, slope, wm, bm, ltri, n_prompt_tiles):
    m = p.shape[0]
    t = ROW_TILE
    nt = m // t

    def sec(k):
        return pl.BlockSpec((t, SEC_W), lambda i, k=k: (i, k))

    def whole(a):
        nd = a.ndim
        return pl.BlockSpec(a.shape, lambda i: (0,) * nd)

    def sel(i):
        return jnp.where(i < n_prompt_tiles, 0, 1)

    row_f32 = lambda w: pl.BlockSpec((t, w), lambda i: (i, 0))
    out_shape = [
        jax.ShapeDtypeStruct((m, SEC_W), F32),
        jax.ShapeDtypeStruct((m, SEC_W), F32),
        jax.ShapeDtypeStruct((m, HEAD_W), F32),
        jax.ShapeDtypeStruct((m, SEC_W), F32),
        jax.ShapeDtypeStruct((m, SEC_W), BF16),
        jax.ShapeDtypeStruct((N_HEADS, 2, HEAD_W + AUG_W, m), BF16),
        jax.ShapeDtypeStruct((N_HEADS, m, HEAD_W + AUG_W), BF16),
        jax.ShapeDtypeStruct((N_HEADS, HEAD_W + ONES_ROWS, m), BF16),
        jax.ShapeDtypeStruct((N_HEADS, 1, HEAD_W + AUG_W, m), BF16),
        jax.ShapeDtypeStruct((N_HEADS, m, HEAD_W + AUG_W), BF16),
        jax.ShapeDtypeStruct((N_HEADS, HEAD_W + ONES_ROWS, m), BF16),
    ]
    qt_spec = lambda n: pl.BlockSpec((N_HEADS, n, HEAD_W + AUG_W, t), lambda i: (0, 0, 0, i))
    k_spec = pl.BlockSpec((N_HEADS, t, HEAD_W + AUG_W), lambda i: (0, i, 0))
    vt_spec = pl.BlockSpec((N_HEADS, HEAD_W + ONES_ROWS, t), lambda i: (0, 0, i))
    out_specs = [
        row_f32(SEC_W), row_f32(SEC_W), row_f32(HEAD_W), row_f32(SEC_W), row_f32(SEC_W),
        qt_spec(2), k_spec, vt_spec, qt_spec(1), k_spec, vt_spec,
    ]
    in_specs = [sec(k) for k in range(N_SEC)] + [
        row_f32(AUG_W),
        whole(gqa), whole(gka), whole(gqb), whole(gkb), whole(gsgu), whole(bf), whole(slope),
        pl.BlockSpec((None, N_HEADS, t, t), lambda i: (sel(i), 0, 0, 0)),
        pl.BlockSpec((None, t, HEAD_W), lambda i: (sel(i), 0, 0)),
        whole(ltri),
    ]
    return pl.pallas_call(
        _post_kernel,
        grid=(nt,),
        in_specs=in_specs,
        out_specs=out_specs,
        out_shape=out_shape,
        scratch_shapes=[pltpu.VMEM((1, HEAD_W), F32)],
        compiler_params=_params("arbitrary"),
        name="post",
    )(*([p] * N_SEC), pf, gqa, gka, gqb, gkb, gsgu, bf, slope, wm, bm, ltri)


def _lam_value(lam_ref, lam_init):
    a = lam_ref[...]
    s1 = jnp.sum(a[0:1] * a[1:2], axis=-1, keepdims=True)
    s2 = jnp.sum(a[2:3] * a[3:4], axis=-1, keepdims=True)
    return jnp.exp(s1) - jnp.exp(s2) + lam_init


def _merge_maps(outs, lam_ref, gsub_ref, lam_init):
    if len(outs) == 1:
        return outs[0]
    o = outs[0] - _lam_value(lam_ref, lam_init) * outs[1]
    return _rms(o, gsub_ref[...]) * (1.0 - lam_init)


FLASH_BLOCK = 1024
FLASH_LANES = 512


def _flash_kernel(qi_tab, ki_tab, qt_ref, k_ref, vt_ref, lam_ref, gsub_ref, o_ref,
                  m_scr, acc_scr, *, n_maps, lam_init):
    s = pl.program_id(1)
    qi = qi_tab[s]
    ki = ki_tab[s]

    @pl.when(ki == 0)
    def _():
        m_scr[...] = jnp.full_like(m_scr, NEG_INF)
        acc_scr[...] = jnp.zeros_like(acc_scr)

    def step(diagonal):
        k = k_ref[...]
        vt = vt_ref[...]
        tb = k.shape[0]
        tq = min(FLASH_LANES, tb)
        work = [(mp, c) for mp in range(n_maps) for c in range(tb // tq)]

        def scores(mp, c):
            return _dot(k, qt_ref[mp, :, c * tq:(c + 1) * tq])

        st_next = scores(*work[0])
        for i, (mp, c) in enumerate(work):
            st = st_next
            if i + 1 < len(work):
                st_next = scores(*work[i + 1])
            cs = slice(c * tq, (c + 1) * tq)
            if diagonal:
                key = lax.broadcasted_iota(jnp.int32, st.shape, 0)
                qry = lax.broadcasted_iota(jnp.int32, st.shape, 1) + c * tq
                st = jnp.where(key <= qry, st, NEG_INF)
            m_prev = m_scr[mp, :, cs]
            m_new = jnp.maximum(m_prev, jnp.max(st, axis=0, keepdims=True))
            p = jnp.exp2(st - m_new)
            acc_scr[mp, :, cs] = (jnp.exp2(m_prev - m_new) * acc_scr[mp, :, cs]
                                  + _dot(vt, p.astype(BF16)))
            m_scr[mp, :, cs] = m_new

    @pl.when(ki < qi)
    def _():
        step(False)

    @pl.when(ki == qi)
    def _():
        step(True)
        outs = []
        for mp in range(n_maps):
            acc = acc_scr[mp]
            outs.append((acc[:HEAD_W] / acc[HEAD_W:HEAD_W + 1]).T)
        o_ref[...] = _merge_maps(outs, lam_ref, gsub_ref, lam_init).astype(o_ref.dtype)


def _flash(qt, k, vt, lam_rows, gsub, n_rows, lam_init):
    n_maps = qt.shape[1]
    v_rows = vt.shape[1]
    tb = _pick_tile(n_rows, (FLASH_BLOCK, 512, 256, 128))
    nq = n_rows // tb
    qi_tab = np.concatenate([np.full(i + 1, i, np.int32) for i in range(nq)])
    ki_tab = np.concatenate([np.arange(i + 1, dtype=np.int32) for i in range(nq)])
    grid_spec = pltpu.PrefetchScalarGridSpec(
        num_scalar_prefetch=2,
        grid=(N_HEADS, len(qi_tab)),
        in_specs=[
            pl.BlockSpec((None, n_maps, HEAD_W + AUG_W, tb), lambda h, s, qt, kt: (h, 0, 0, qt[s])),
            pl.BlockSpec((None, tb, HEAD_W + AUG_W), lambda h, s, qt, kt: (h, kt[s], 0)),
            pl.BlockSpec((None, v_rows, tb), lambda h, s, qt, kt: (h, 0, kt[s])),
            pl.BlockSpec(lam_rows.shape, lambda h, s, qt, kt: (0, 0)),
            pl.BlockSpec(gsub.shape, lambda h, s, qt, kt: (0, 0)),
        ],
        out_specs=pl.BlockSpec((tb, HEAD_W), lambda h, s, qt, kt: (qt[s], h)),
        scratch_shapes=[
            pltpu.VMEM((n_maps, 1, tb), F32),
            pltpu.VMEM((n_maps, v_rows, tb), F32),
        ],
    )
    return pl.pallas_call(
        functools.partial(_flash_kernel, n_maps=n_maps, lam_init=lam_init),
        grid_spec=grid_spec,
        out_shape=jax.ShapeDtypeStruct((n_rows, SEC_W), BF16),
        compiler_params=_params("parallel", "arbitrary"),
        name=f"flash{n_maps}",
    )(jnp.asarray(qi_tab), jnp.asarray(ki_tab), qt, k, vt, lam_rows, gsub)


def _pagecum_kernel(x_ref, u_ref, ut_ref, wc_ref, wt_ref):
    hi, mid, lo = (a.astype(BF16) for a in _split3(x_ref[...]))
    u = u_ref[...]
    ut = ut_ref[...]
    wc_ref[...] = _dot(hi, u) + _dot(mid, u) + _dot(lo, u)
    wt_ref[...] = _dot(hi, ut) + _dot(mid, ut) + _dot(lo, ut)


def _pagecum(x, u, ut):
    r = x.shape[0]
    tr = _pick_tile(r, (512, 256, 128, 64, 32, 16, 8))
    spec = pl.BlockSpec((tr, PAGE_FLAT), lambda i: (i, 0))
    wspec = pl.BlockSpec((PAGE_FLAT, PAGE_FLAT), lambda i: (0, 0))
    return pl.pallas_call(
        _pagecum_kernel,
        grid=(r // tr,),
        in_specs=[spec, wspec, wspec],
        out_specs=[spec, spec],
        out_shape=[jax.ShapeDtypeStruct((r, PAGE_FLAT), F32)] * 2,
        compiler_params=_params("parallel"),
        name="pagecum",
    )(x, u, ut)


PAGES_PER_STEP = 4


def _softmax_update(scores, pv_fn, m_scr, l_scr, acc_scr):
    m_prev = m_scr[...]
    m_tile = functools.reduce(jnp.maximum, scores)
    m_new = jnp.maximum(m_prev, jnp.max(m_tile, axis=-1, keepdims=True))
    alpha = jnp.exp2(m_prev - m_new)
    ps = [jnp.exp2(sc - m_new) for sc in scores]
    row_sum = jnp.sum(ps[0], axis=-1, keepdims=True)
    for p in ps[1:]:
        row_sum = row_sum + jnp.sum(p, axis=-1, keepdims=True)
    l_scr[...] = alpha * l_scr[...] + row_sum
    acc_scr[...] = alpha * acc_scr[...] + pv_fn(ps)
    m_scr[...] = m_new


def _decode_forget_kernel(pt_ref, *refs, n_pages):
    pps = PAGES_PER_STEP
    q_ref = refs[0]
    k_refs = refs[1:1 + pps]
    v_refs = refs[1 + pps:1 + 2 * pps]
    wc_refs = refs[1 + 2 * pps:1 + 3 * pps]
    wt_refs = refs[1 + 3 * pps:1 + 4 * pps]
    cnew_ref, knew_ref, vnew_ref, o_ref, m_scr, l_scr, acc_scr, carry_scr = refs[1 + 4 * pps:]

    step = pl.program_id(1)
    n_steps = n_pages // pps
    q = q_ref[...]
    r = q.shape[0]

    @pl.when(step == 0)
    def _():
        m_scr[...] = jnp.full_like(m_scr, NEG_INF)
        l_scr[...] = jnp.zeros_like(l_scr)
        acc_scr[...] = jnp.zeros_like(acc_scr)
        carry_scr[...] = jnp.zeros_like(carry_scr)

    def scores(k, bias, causal):
        c = k.shape[0]
        sc = _dot_nt(q, k.astype(BF16)) + bias
        row = lax.broadcasted_iota(jnp.int32, (r, c), 0)
        col = lax.broadcasted_iota(jnp.int32, (r, c), 1)
        ok = (col & (N_HEADS - 1)) == ((row >> 3) & (N_HEADS - 1))
        if causal:
            ok = ok & ((col >> 3) <= (row & 7))
        return jnp.where(ok, sc, NEG_INF)

    @pl.when(step < n_steps)
    def _():
        carry = carry_scr[...]
        tiles = []
        for j in range(pps):
            tiles.append(scores(k_refs[j][...], (carry + wc_refs[j][...]) * (-LOG2E), False))
            carry = carry + wt_refs[j][...]
        carry_scr[...] = carry
        pv = lambda ps: sum(_dot(p.astype(BF16), v_refs[j][...].astype(BF16)) for j, p in enumerate(ps))
        _softmax_update(tiles, pv, m_scr, l_scr, acc_scr)

    @pl.when(step == n_steps)
    def _():
        c = knew_ref.shape[0]
        bias = (carry_scr[:, :c] + cnew_ref[:, :c]) * (-LOG2E)
        tile = scores(knew_ref[...], bias, True)
        pv = lambda ps: _dot(ps[0].astype(BF16), vnew_ref[...].astype(BF16))
        _softmax_update([tile], pv, m_scr, l_scr, acc_scr)
        o_ref[...] = acc_scr[...] / l_scr[...]


def _decode_diff_kernel(pt_ref, *refs, n_pages, lam_init):
    pps = PAGES_PER_STEP
    q_ref = refs[0]
    kt_refs = refs[1:1 + pps]
    v_refs = refs[1 + pps:1 + 2 * pps]
    (slope_ref, knew_ref, vnew_ref, lam_ref, gsub_ref, o_ref,
     m_scr, l_scr, acc_scr, kpad_scr, vpad_scr) = refs[1 + 2 * pps:]

    step = pl.program_id(1)
    n_steps = n_pages // pps
    q = q_ref[...]
    r = q.shape[0]
    rows_per_head = r // N_HEADS
    n_new = knew_ref.shape[0]

    @pl.when(step == 0)
    def _():
        m_scr[...] = jnp.full_like(m_scr, NEG_INF)
        l_scr[...] = jnp.zeros_like(l_scr)
        acc_scr[...] = jnp.zeros_like(acc_scr)

    tok = lax.broadcasted_iota(jnp.int32, (1, PAGE), 1)

    def head_pv(ps, value_of):
        outs = []
        for h in range(N_HEADS):
            rs = slice(h * rows_per_head, (h + 1) * rows_per_head)
            outs.append(sum(_dot(p[rs].astype(BF16), value_of(j, h)) for j, p in enumerate(ps)))
        return jnp.concatenate(outs, axis=0)

    @pl.when(step < n_steps)
    def _():
        tiles = []
        for j in range(pps):
            pos = (tok + (step * pps + j) * PAGE).astype(F32)
            tiles.append(_dot(q, kt_refs[j][...].astype(BF16)) + slope_ref[...] * pos)
        value_of = lambda j, h: v_refs[j][pl.ds(h, PAGE, stride=N_HEADS), :].astype(BF16)
        _softmax_update(tiles, lambda ps: head_pv(ps, value_of), m_scr, l_scr, acc_scr)

    @pl.when(step == n_steps)
    def _():
        kpad_scr[...] = jnp.zeros_like(kpad_scr)
        vpad_scr[...] = jnp.zeros_like(vpad_scr)
        kpad_scr[0:n_new, :] = knew_ref[...]
        vpad_scr[0:n_new, :] = vnew_ref[...]
        pos = (tok + n_pages * PAGE).astype(F32)
        sc = _dot_nt(q, kpad_scr[...].astype(BF16)) + slope_ref[...] * pos
        q_idx = lax.broadcasted_iota(jnp.int32, (r, PAGE), 0) & 7
        sc = jnp.where(tok <= q_idx, sc, NEG_INF)
        value_of = lambda j, h: vpad_scr[:, h * HEAD_W:(h + 1) * HEAD_W].astype(BF16)
        _softmax_update([sc], lambda ps: head_pv(ps, value_of), m_scr, l_scr, acc_scr)
        o = (acc_scr[...] / l_scr[...]).reshape(N_HEADS, 2, rows_per_head // 2, HEAD_W)
        outs = [o[:, mp].reshape(r // 2, HEAD_W) for mp in range(2)]
        o_ref[...] = _merge_maps(outs, lam_ref, gsub_ref, lam_init)


def _decode_specs(n_pages, layer):
    pps = PAGES_PER_STEP
    assert n_pages % pps == 0
    n_steps = n_pages // pps

    def page_idx(s, j):
        return jnp.minimum(s, n_steps - 1) * pps + j

    def cache_spec(j):
        return pl.BlockSpec((None, None, PAGE_FLAT, HEAD_W),
                            lambda bi, s, pt, j=j: (layer, pt[bi, page_idx(s, j)], 0, 0))

    def cum_spec(j):
        return pl.BlockSpec((None, 1, PAGE_FLAT), lambda bi, s, pt, j=j: (pt[bi, page_idx(s, j)], 0, 0))

    per_seq = lambda shape: pl.BlockSpec((None,) + shape, lambda bi, s, pt: (bi, 0, 0))
    whole2 = lambda a: pl.BlockSpec(a.shape, lambda bi, s, pt: (0, 0))
    return n_steps, cache_spec, cum_spec, per_seq, whole2


def _decode_forget(page_table, q, k_cache, v_cache, layer, knew, vnew, wc, wt, cnew):
    b, r, _ = q.shape
    n_pages = page_table.shape[1]
    pps = PAGES_PER_STEP
    n_steps, cache_spec, cum_spec, per_seq, _ = _decode_specs(n_pages, layer)
    c_new = knew.shape[1]
    pages = lambda spec: [spec(j) for j in range(pps)]
    in_specs = ([per_seq((r, HEAD_W))] + pages(cache_spec) + pages(cache_spec) + pages(cum_spec)
                + pages(cum_spec) + [per_seq((1, PAGE_FLAT)), per_seq((c_new, HEAD_W)),
                                     per_seq((c_new, HEAD_W))])
    args = [q] + [k_cache] * pps + [v_cache] * pps + [wc] * pps + [wt] * pps + [cnew, knew, vnew]
    grid_spec = pltpu.PrefetchScalarGridSpec(
        num_scalar_prefetch=1,
        grid=(b, n_steps + 1),
        in_specs=in_specs,
        out_specs=per_seq((r, HEAD_W)),
        scratch_shapes=[
            pltpu.VMEM((r, 1), F32),
            pltpu.VMEM((r, 1), F32),
            pltpu.VMEM((r, HEAD_W), F32),
            pltpu.VMEM((1, PAGE_FLAT), F32),
        ],
    )
    return pl.pallas_call(
        functools.partial(_decode_forget_kernel, n_pages=n_pages),
        grid_spec=grid_spec,
        out_shape=jax.ShapeDtypeStruct((b, r, HEAD_W), F32),
        compiler_params=_params("parallel", "arbitrary"),
        name="decode_forget",
    )(page_table, *args)


def _decode_diff(page_table, q, kt_cache, v_cache, layer, knew, vnew, slope_rows, lam_rows, gsub,
                 lam_init):
    b, r, _ = q.shape
    n_pages = page_table.shape[1]
    pps = PAGES_PER_STEP
    n_steps, cache_spec, _, per_seq, whole2 = _decode_specs(n_pages, layer)
    n_new = knew.shape[1]
    pages = lambda spec: [spec(j) for j in range(pps)]
    in_specs = ([per_seq((r, SEC_W))] + pages(cache_spec) + pages(cache_spec)
                + [whole2(slope_rows), per_seq((n_new, SEC_W)), per_seq((n_new, SEC_W)),
                   whole2(lam_rows), whole2(gsub)])
    args = [q] + [kt_cache] * pps + [v_cache] * pps + [slope_rows, knew, vnew, lam_rows, gsub]
    grid_spec = pltpu.PrefetchScalarGridSpec(
        num_scalar_prefetch=1,
        grid=(b, n_steps + 1),
        in_specs=in_specs,
        out_specs=per_seq((r // 2, HEAD_W)),
        scratch_shapes=[
            pltpu.VMEM((r, 1), F32),
            pltpu.VMEM((r, 1), F32),
            pltpu.VMEM((r, HEAD_W), F32),
            pltpu.VMEM((PAGE, SEC_W), F32),
            pltpu.VMEM((PAGE, SEC_W), F32),
        ],
    )
    return pl.pallas_call(
        functools.partial(_decode_diff_kernel, n_pages=n_pages, lam_init=lam_init),
        grid_spec=grid_spec,
        out_shape=jax.ShapeDtypeStruct((b, r // 2, HEAD_W), F32),
        compiler_params=_params("parallel", "arbitrary"),
        name="decode_diff",
    )(page_table, *args)


def _zmerge_kernel(h_ref, oa_ref, ob_ref, oc_ref, wa_ref, wb_ref, wc_ref,
                   wga_ref, wgb_ref, wgc_ref, bga_ref, bgb_ref, bgc_ref, z_ref):
    h = h_ref[...]
    ga = jax.nn.sigmoid(_dot(h, wga_ref[...]) + bga_ref[...])
    gb = jax.nn.sigmoid(_dot(h, wgb_ref[...]) + bgb_ref[...])
    gc = jax.nn.sigmoid(_dot(h, wgc_ref[...]) + bgc_ref[...])
    z = (ga * _dot(oa_ref[...], wa_ref[...]) + gb * _dot(ob_ref[...], wb_ref[...])
         + gc * _dot(oc_ref[...], wc_ref[...]))
    z_ref[...] = z.astype(z_ref.dtype)


def _zmerge(h, oa, ob, oc, wa, wb, wc, wg, bg):
    m, d = h.shape
    tm = _pick_tile(m, (768, 512, 256))
    tn = _pick_tile(d, (512, 256))
    nn = d // tn
    rows = lambda w: pl.BlockSpec((tm, w), lambda i, j: (i, 0))
    wcol = lambda kdim, off: pl.BlockSpec((kdim, tn), lambda i, j, off=off: (0, j + off))
    bcol = lambda off: pl.BlockSpec((1, tn), lambda i, j, off=off: (0, j + off))
    return pl.pallas_call(
        _zmerge_kernel,
        grid=(m // tm, nn),
        in_specs=[rows(d), rows(SEC_W), rows(SEC_W), rows(SEC_W),
                  wcol(SEC_W, 0), wcol(SEC_W, 0), wcol(SEC_W, 0),
                  wcol(d, 0), wcol(d, nn), wcol(d, 2 * nn),
                  bcol(0), bcol(nn), bcol(2 * nn)],
        out_specs=pl.BlockSpec((tm, tn), lambda i, j: (i, j)),
        out_shape=jax.ShapeDtypeStruct((m, d), BF16),
        compiler_params=_params("parallel", "arbitrary"),
        name="zmerge",
    )(h, oa, ob, oc, wa, wb, wc, wg, wg, wg, bg, bg, bg)


def _outproj_kernel(x_ref, z_ref, w_ref, o_ref):
    o_ref[...] = x_ref[...] + _dot(z_ref[...], w_ref[...])


def _outproj(x, z, w):
    m, d = x.shape
    tm = _pick_tile(m, (768, 512, 256))
    tn = _pick_tile(d, (1024, 512))
    return pl.pallas_call(
        _outproj_kernel,
        grid=(m // tm, d // tn),
        in_specs=[pl.BlockSpec((tm, tn), lambda i, j: (i, j)),
                  pl.BlockSpec((tm, d), lambda i, j: (i, 0)),
                  pl.BlockSpec((d, tn), lambda i, j: (0, j))],
        out_specs=pl.BlockSpec((tm, tn), lambda i, j: (i, j)),
        out_shape=jax.ShapeDtypeStruct((m, d), F32),
        compiler_params=_params("parallel", "arbitrary"),
        name="outproj",
    )(x, z, w)


def _ffn_kernel(x_ref, g_ref, wg_ref, wu_ref, wd_ref, o_ref, h_scr):
    @pl.when(pl.program_id(1) == 0)
    def _():
        x = x_ref[...]
        h_scr[...] = _rms(x, g_ref[...]).astype(BF16)
        o_ref[...] = x

    h = h_scr[...]
    a = jax.nn.silu(_dot(h, wg_ref[...])) * _dot(h, wu_ref[...])
    o_ref[...] += _dot(a.astype(BF16), wd_ref[...])


def _ffn(x, g, wg, wu, wd):
    m, d = x.shape
    f = wg.shape[1]
    tm = _pick_tile(m, (768, 512, 256))
    tf = _pick_tile(f, (512, 256))
    return pl.pallas_call(
        _ffn_kernel,
        grid=(m // tm, f // tf),
        in_specs=[pl.BlockSpec((tm, d), lambda i, j: (i, 0)),
                  pl.BlockSpec((1, d), lambda i, j: (0, 0)),
                  pl.BlockSpec((d, tf), lambda i, j: (0, j)),
                  pl.BlockSpec((d, tf), lambda i, j: (0, j)),
                  pl.BlockSpec((tf, d), lambda i, j: (j, 0))],
        out_specs=pl.BlockSpec((tm, d), lambda i, j: (i, 0)),
        out_shape=jax.ShapeDtypeStruct((m, d), F32),
        scratch_shapes=[pltpu.VMEM((tm, d), BF16)],
        compiler_params=_params("parallel", "arbitrary"),
        name="ffn",
    )(x, g, wg, wu, wd)


def _page_cum_matrices():
    idx = np.arange(PAGE_FLAT)
    tok, head = idx >> 3, idx & 7
    same = head[:, None] == head[None, :]
    u = same & (tok[:, None] <= tok[None, :])
    return jnp.asarray(u, BF16), jnp.asarray(same, BF16)


def _mix_matrices(w_spatial, b_spatial, n_sample_tok):
    t = ROW_TILE
    tril = jnp.where(jnp.tril(jnp.ones((CHUNK, CHUNK), bool)), w_spatial, 0.0)
    eye_p = jnp.eye(t // CHUNK, dtype=F32)
    wm_p = jnp.einsum("ab,gts->gatbs", eye_p, tril).reshape(N_HEADS, t, t)
    bm_p = jnp.tile(b_spatial, (1, t // CHUNK))
    n = n_sample_tok
    eye_s = jnp.eye(t // n, dtype=F32)
    wm_s = jnp.einsum("ab,gts->gatbs", eye_s, tril[:, :n, :n]).reshape(N_HEADS, t, t)
    bm_s = jnp.tile(b_spatial[:, :n], (1, t // n))
    wm = jnp.stack([wm_p, wm_s]).astype(BF16)
    bm = jnp.stack([bm_p, bm_s])
    bm = jnp.pad(jnp.swapaxes(bm, 1, 2), ((0, 0), (0, 0), (0, HEAD_W - N_HEADS)))
    return wm, bm


def _lane_row(v, width=HEAD_W):
    v = v.reshape(1, -1)
    return jnp.pad(v, ((0, 0), (0, width - v.shape[1])))


def kernel(x_prompt, x_sample, cache_diff_k, cache_diff_v, cache_fox_k, cache_fox_v, cache_fox_logf,
           page_table, g_mix, w_in, b_forget, g_q_a, g_k_a, lam_q1, lam_k1, lam_q2, lam_k2, g_subln,
           g_q_b, g_k_b, g_sgu, w_spatial, b_spatial, w_branch_a, w_branch_b, w_branch_c, w_gate,
           b_gate, w_out, g_ffn, w_ffn_gate, w_ffn_up, w_ffn_down):
    depth = g_mix.shape[0]
    n_batch, seq, d_model = x_prompt.shape
    dec_b, dec_t, _ = x_sample.shape
    n_phys = cache_diff_k.shape[1]
    n_dec = dec_b * dec_t
    assert n_batch == 1 and d_model == N_HEADS * 2 * HEAD_W
    assert seq % ROW_TILE == 0 and n_dec == ROW_TILE and dec_t == 8
    assert cache_diff_k.shape[2] == PAGE
    m = seq + n_dec

    x = jnp.concatenate([x_prompt.reshape(seq, d_model), x_sample.reshape(n_dec, d_model)], axis=0)

    cdk = jnp.transpose(cache_diff_k, (0, 1, 3, 4, 5, 2)).reshape(depth, n_phys, PAGE_FLAT, PAGE)
    cdv = cache_diff_v.reshape(depth, n_phys, PAGE_FLAT, HEAD_W)
    cfk = cache_fox_k.reshape(depth, n_phys, PAGE_FLAT, HEAD_W)
    cfv = cache_fox_v.reshape(depth, n_phys, PAGE_FLAT, HEAD_W)
    cfl = cache_fox_logf.reshape(depth, n_phys, PAGE_FLAT)

    slopes = np.exp2(-8.0 * np.arange(1, N_HEADS + 1) / N_HEADS).astype(np.float32)
    slope_row = jnp.asarray(np.pad(slopes, (0, HEAD_W - N_HEADS)).reshape(1, HEAD_W))
    slope_rows = jnp.asarray(np.broadcast_to(np.repeat(slopes * np.float32(LOG2E), 2 * dec_t)[:, None],
                                             (2 * N_HEADS * dec_t, PAGE)))
    head_block = jnp.asarray((np.arange(SEC_W) // HEAD_W)[None, :] == np.arange(N_HEADS)[:, None]
                             ).reshape(1, N_HEADS, 1, 1, SEC_W)
    ltri = jnp.asarray(np.tril(np.ones((ROW_TILE, ROW_TILE), np.float32)), BF16)
    u_cum, u_tot = _page_cum_matrices()
    widths = np.cumsum([SEC_W] * 6 + [N_HEADS])

    rows_prompt, rows_sample = [], []
    for layer in range(depth):
        lam_init = 0.8 - 0.6 * math.exp(-0.3 * layer)
        w_l = w_in[layer]
        w_main = jnp.concatenate([w_l[:, :widths[5]], w_l[:, widths[6]:]], axis=1).astype(BF16)
        w_f = jnp.pad(w_l[:, widths[5]:widths[6]], ((0, 0), (0, AUG_W - N_HEADS))).astype(BF16)
        p, pf, h = _proj_in(x, g_mix[layer].reshape(1, d_model), w_main, w_f)

        wm, bm = _mix_matrices(w_spatial[layer], b_spatial[layer], dec_t)
        lam_rows = jnp.pad(jnp.stack([lam_q1[layer], lam_k1[layer], lam_q2[layer], lam_k2[layer]]),
                           ((0, 4), (0, HEAD_W - MAP_W)))
        gsub = g_subln[layer].reshape(1, HEAD_W)
        (kan, kbn, logf, vcn, oc, qa_o, ka_o, va_o, qb_o, kb_o, vb_o) = _post(
            p, pf,
            jnp.tile(g_q_a[layer], 2).reshape(1, HEAD_W), jnp.tile(g_k_a[layer], 2).reshape(1, HEAD_W),
            g_q_b[layer].reshape(1, HEAD_W), g_k_b[layer].reshape(1, HEAD_W),
            g_sgu[layer].reshape(1, SEC_W), _lane_row(b_forget[layer]), slope_row,
            wm, bm, ltri, seq // ROW_TILE)
        va = p[:, 2 * SEC_W:3 * SEC_W]
        vb = p[:, 5 * SEC_W:6 * SEC_W]

        oa_p = _flash(qa_o, ka_o, va_o, lam_rows, gsub, seq, lam_init)
        ob_p = _flash(qb_o, kb_o, vb_o, lam_rows, gsub, seq, lam_init)

        to_rows = lambda a: a[seq:].reshape(dec_b, dec_t * N_HEADS, HEAD_W)
        per_seq = lambda a: a[seq:].reshape(dec_b, dec_t, SEC_W)
        q_hm = jnp.transpose(qa_o[:, :, :HEAD_W, seq:].reshape(N_HEADS, 2, HEAD_W, dec_b, dec_t),
                             (3, 0, 1, 4, 2))
        q_a = jnp.where(head_block, jnp.tile(q_hm, (1, 1, 1, 1, N_HEADS)), 0).reshape(
            dec_b, 2 * N_HEADS * dec_t, SEC_W)
        q_b = jnp.transpose(qb_o[:, 0, :HEAD_W, seq:].reshape(N_HEADS, HEAD_W, dec_b, dec_t),
                            (2, 0, 3, 1)).reshape(dec_b, N_HEADS * dec_t, HEAD_W)
        oa_s = _decode_diff(page_table, q_a, cdk, cdv, layer, per_seq(kan), per_seq(va), slope_rows,
                            lam_rows, gsub, lam_init)
        wc, wt = _pagecum(cfl[layer], u_cum, u_tot)
        lf_new = jnp.pad(logf[seq:, :N_HEADS].reshape(dec_b, dec_t * N_HEADS),
                         ((0, 0), (0, PAGE_FLAT - dec_t * N_HEADS)))
        cnew, _ = _pagecum(lf_new, u_cum, u_tot)
        ob_s = _decode_forget(page_table, q_b, cfk, cfv, layer, to_rows(kbn), to_rows(vb),
                              wc.reshape(n_phys, 1, PAGE_FLAT), wt.reshape(n_phys, 1, PAGE_FLAT),
                              cnew.reshape(dec_b, 1, PAGE_FLAT))
        from_rows = lambda o: jnp.transpose(o.reshape(dec_b, N_HEADS, dec_t, HEAD_W),
                                            (0, 2, 1, 3)).reshape(n_dec, SEC_W).astype(BF16)
        oa = jnp.concatenate([oa_p, from_rows(oa_s)], axis=0)
        ob = jnp.concatenate([ob_p, from_rows(ob_s)], axis=0)

        z = _zmerge(h, oa, ob, oc, w_branch_a[layer].astype(BF16), w_branch_b[layer].astype(BF16),
                    w_branch_c[layer].astype(BF16), w_gate[layer].astype(BF16),
                    b_gate[layer].reshape(1, -1))
        x = _outproj(x, z, w_out[layer].astype(BF16))
        x = _ffn(x, g_ffn[layer].reshape(1, d_model), w_ffn_gate[layer].astype(BF16),
                 w_ffn_up[layer].astype(BF16), w_ffn_down[layer].astype(BF16))

        lf = logf[:, :N_HEADS]
        rows_prompt.append((kan[:seq], va[:seq], kbn[:seq], vb[:seq], lf[:seq]))
        rows_sample.append((kan[seq:], va[seq:], kbn[seq:], vb[seq:], lf[seq:], vcn[seq:]))

    stack = lambda rows, i, shape: jnp.stack([r[i] for r in rows], axis=0).reshape((depth,) + shape)
    p_shape, s_shape = (n_batch, seq), (dec_b, dec_t)
    return (
        x[:seq].reshape(n_batch, seq, d_model),
        x[seq:].reshape(dec_b, dec_t, d_model),
        stack(rows_prompt, 0, p_shape + (N_HEADS, 2, MAP_W)),
        stack(rows_prompt, 1, p_shape + (N_HEADS, HEAD_W)),
        stack(rows_prompt, 2, p_shape + (N_HEADS, HEAD_W)),
        stack(rows_prompt, 3, p_shape + (N_HEADS, HEAD_W)),
        stack(rows_prompt, 4, p_shape + (N_HEADS,)),
        stack(rows_sample, 0, s_shape + (N_HEADS, 2, MAP_W)),
        stack(rows_sample, 1, s_shape + (N_HEADS, HEAD_W)),
        stack(rows_sample, 2, s_shape + (N_HEADS, HEAD_W)),
        stack(rows_sample, 3, s_shape + (N_HEADS, HEAD_W)),
        stack(rows_sample, 4, s_shape + (N_HEADS,)),
        stack(rows_sample, 5, s_shape + (SEC_W,)),
    )
```

```python
import functools
import math

import numpy as np
import jax
import jax.numpy as jnp
from jax import lax
from jax.experimental import pallas as pl
from jax.experimental.pallas import tpu as pltpu

F32 = jnp.float32
BF16 = jnp.bfloat16

N_HEADS = 8
HEAD_W = 128
MAP_W = 64
SEC_W = N_HEADS * HEAD_W
N_SEC = 8
CHUNK = 128
PAGE = 128
PAGE_FLAT = PAGE * N_HEADS
AUG_W = 128
ONES_ROWS = 16
LOG2E = 1.4426950408889634
EPS = 1e-6
NEG_INF = float("-inf")

MXU_DEPTH = 256
ROW_TILE = MXU_DEPTH
VMEM_LIMIT = 56 * 1024 * 1024


def _params(*sem):
    return pltpu.CompilerParams(dimension_semantics=sem, vmem_limit_bytes=VMEM_LIMIT)


def _pick_tile(n, candidates):
    for c in candidates:
        if n % c == 0:
            return c
    raise ValueError(f"no tile in {candidates} divides {n}")


def _rms(x, g):
    ms = jnp.mean(x * x, axis=-1, keepdims=True)
    return x * lax.rsqrt(ms + EPS) * g


def _split3(x):
    hi = x.astype(BF16).astype(F32)
    r = x - hi
    mid = r.astype(BF16).astype(F32)
    lo = (r - mid).astype(BF16).astype(F32)
    return hi, mid, lo


def _gelu(x):
    return 0.5 * x * (1.0 + lax.erf(x * (2.0 ** -0.5)))


def _dot(a, b):
    return jnp.dot(a, b, preferred_element_type=F32)


def _dot_nt(a, b):
    return lax.dot_general(a, b, (((1,), (1,)), ((), ())), preferred_element_type=F32)


def _proj_in_kernel(x_ref, g_ref, w_ref, wf_ref, p_ref, pf_ref, h_ref):
    @pl.when(pl.program_id(1) == 0)
    def _():
        hb = _rms(x_ref[...], g_ref[...]).astype(BF16)
        h_ref[...] = hb
        pf_ref[...] = _dot(hb, wf_ref[...])

    p_ref[...] = _dot(h_ref[...], w_ref[...])


def _proj_in(x, g, w_main, w_f):
    m, d = x.shape
    n = w_main.shape[1]
    tm = _pick_tile(m, (768, 512, 256))
    tn = _pick_tile(n, (1024, 512))
    return pl.pallas_call(
        _proj_in_kernel,
        grid=(m // tm, n // tn),
        in_specs=[
            pl.BlockSpec((tm, d), lambda i, j: (i, 0)),
            pl.BlockSpec((1, d), lambda i, j: (0, 0)),
            pl.BlockSpec((d, tn), lambda i, j: (0, j)),
            pl.BlockSpec((d, AUG_W), lambda i, j: (0, 0)),
        ],
        out_specs=[
            pl.BlockSpec((tm, tn), lambda i, j: (i, j)),
            pl.BlockSpec((tm, AUG_W), lambda i, j: (i, 0)),
            pl.BlockSpec((tm, d), lambda i, j: (i, 0)),
        ],
        out_shape=[
            jax.ShapeDtypeStruct((m, n), F32),
            jax.ShapeDtypeStruct((m, AUG_W), F32),
            jax.ShapeDtypeStruct((m, d), BF16),
        ],
        compiler_params=_params("parallel", "arbitrary"),
        name="proj_in",
    )(x, g, w_main, w_f)


def _norm_halves(xh, g):
    lo = lax.broadcasted_iota(jnp.int32, xh.shape, 1) < MAP_W
    sq = xh * xh
    s_lo = jnp.sum(jnp.where(lo, sq, 0.0), axis=-1, keepdims=True)
    s_hi = jnp.sum(jnp.where(lo, 0.0, sq), axis=-1, keepdims=True)
    ms = jnp.where(lo, s_lo, s_hi) * (1.0 / MAP_W)
    return xh * lax.rsqrt(ms + EPS) * g


def _post_kernel(qa_ref, ka_ref, va_ref, qb_ref, kb_ref, vb_ref, uc_ref, vc_ref, pf_ref,
                 gqa_ref, gka_ref, gqb_ref, gkb_ref, gsgu_ref, bf_ref, slope_ref,
                 wm_ref, bm_ref, ltri_ref,
                 kan_ref, kbn_ref, logf_ref, vcn_ref, oc_ref,
                 qa_o, ka_o, va_o, qb_o, kb_o, vb_o,
                 carry_ref):
    i = pl.program_id(0)
    t = qa_ref.shape[0]
    lane = lax.broadcasted_iota(jnp.int32, (t, HEAD_W), 1)
    row = lax.broadcasted_iota(jnp.int32, (t, HEAD_W), 0) + i * t

    @pl.when(i == 0)
    def _():
        carry_ref[...] = jnp.zeros_like(carry_ref)

    logf = jnp.where(lane < N_HEADS, jax.nn.log_sigmoid(pf_ref[...] + bf_ref[...]), 0.0)
    logf_ref[...] = logf
    ltri = ltri_ref[...]
    hi, mid, lo = _split3(logf)
    cum = (_dot(ltri, hi.astype(BF16)) + _dot(ltri, mid.astype(BF16)) + _dot(ltri, lo.astype(BF16))
           + carry_ref[...])
    carry_ref[...] = cum[t - 1:t, :]

    gl = _gelu(vc_ref[...])
    vcn = _rms(gl, gsgu_ref[...])
    vcn_ref[...] = vcn
    bm = bm_ref[...]

    feat = lax.broadcasted_iota(jnp.int32, (HEAD_W, t), 0)
    lo_feat = feat < MAP_W
    ones_q = jnp.where(feat < 3, 1.0, 0.0)
    ones_v = jnp.ones((ONES_ROWS, t), F32)
    pos = row.astype(F32)

    def key_term(x):
        hi, mid, lo = _split3(x)
        return jnp.where(lane == 0, hi, jnp.where(lane == 1, mid, jnp.where(lane == 2, lo, 0.0)))

    for h in range(N_HEADS):
        cs = slice(h * HEAD_W, (h + 1) * HEAD_W)
        mixed = _dot(wm_ref[h], vcn[:, cs].astype(BF16)) + bm[:, h:h + 1]
        oc_ref[:, cs] = (_gelu(uc_ref[:, cs]) * mixed).astype(BF16)

        qt = (_norm_halves(qa_ref[:, cs], gqa_ref[...]) * (MAP_W ** -0.5 * LOG2E)).T
        kn = _norm_halves(ka_ref[:, cs], gka_ref[...])
        kan_ref[:, cs] = kn
        qa_o[h, 0] = jnp.concatenate([jnp.where(lo_feat, qt, 0.0), ones_q], axis=0).astype(BF16)
        qa_o[h, 1] = jnp.concatenate([jnp.where(lo_feat, 0.0, qt), ones_q], axis=0).astype(BF16)
        aug = key_term(pos[:, :1] * (slope_ref[:, h:h + 1] * LOG2E))
        ka_o[h] = jnp.concatenate([kn, aug], axis=-1).astype(BF16)
        va_o[h] = jnp.concatenate([va_ref[:, cs].T, ones_v], axis=0).astype(BF16)

        qt = (_rms(qb_ref[:, cs], gqb_ref[...]) * (HEAD_W ** -0.5 * LOG2E)).T
        kn = _rms(kb_ref[:, cs], gkb_ref[...])
        kbn_ref[:, cs] = kn
        qb_o[h, 0] = jnp.concatenate([qt, ones_q], axis=0).astype(BF16)
        aug = key_term(cum[:, h:h + 1] * (-LOG2E))
        kb_o[h] = jnp.concatenate([kn, aug], axis=-1).astype(BF16)
        vb_o[h] = jnp.concatenate([vb_ref[:, cs].T, ones_v], axis=0).astype(BF16)


def _post(p, pf, gqa, gka, gqb, gkb, gsgu, bf, slope, wm, bm, ltri, n_prompt_tiles):
    m = p.shape[0]
    t = ROW_TILE
    nt = m // t

    def sec(k):
        return pl.BlockSpec((t, SEC_W), lambda i, k=k: (i, k))

    def whole(a):
        nd = a.ndim
        return pl.BlockSpec(a.shape, lambda i: (0,) * nd)

    def sel(i):
        return jnp.where(i < n_prompt_tiles, 0, 1)

    row_f32 = lambda w: pl.BlockSpec((t, w), lambda i: (i, 0))
    out_shape = [
        jax.ShapeDtypeStruct((m, SEC_W), F32),
        jax.ShapeDtypeStruct((m, SEC_W), F32),
        jax.ShapeDtypeStruct((m, HEAD_W), F32),
        jax.ShapeDtypeStruct((m, SEC_W), F32),
        jax.ShapeDtypeStruct((m, SEC_W), BF16),
        jax.ShapeDtypeStruct((N_HEADS, 2, HEAD_W + AUG_W, m), BF16),
        jax.ShapeDtypeStruct((N_HEADS, m, HEAD_W + AUG_W), BF16),
        jax.ShapeDtypeStruct((N_HEADS, HEAD_W + ONES_ROWS, m), BF16),
        jax.ShapeDtypeStruct((N_HEADS, 1, HEAD_W + AUG_W, m), BF16),
        jax.ShapeDtypeStruct((N_HEADS, m, HEAD_W + AUG_W), BF16),
        jax.ShapeDtypeStruct((N_HEADS, HEAD_W + ONES_ROWS, m), BF16),
    ]
    qt_spec = lambda n: pl.BlockSpec((N_HEADS, n, HEAD_W + AUG_W, t), lambda i: (0, 0, 0, i))
    k_spec = pl.BlockSpec((N_HEADS, t, HEAD_W + AUG_W), lambda i: (0, i, 0))
    vt_spec = pl.BlockSpec((N_HEADS, HEAD_W + ONES_ROWS, t), lambda i: (0, 0, i))
    out_specs = [
        row_f32(SEC_W), row_f32(SEC_W), row_f32(HEAD_W), row_f32(SEC_W), row_f32(SEC_W),
        qt_spec(2), k_spec, vt_spec, qt_spec(1), k_spec, vt_spec,
    ]
    in_specs = [sec(k) for k in range(N_SEC)] + [
        row_f32(AUG_W),
        whole(gqa), whole(gka), whole(gqb), whole(gkb), whole(gsgu), whole(bf), whole(slope),
        pl.BlockSpec((None, N_HEADS, t, t), lambda i: (sel(i), 0, 0, 0)),
        pl.BlockSpec((None, t, HEAD_W), lambda i: (sel(i), 0, 0)),
        whole(ltri),
    ]
    return pl.pallas_call(
        _post_kernel,
        grid=(nt,),
        in_specs=in_specs,
        out_specs=out_specs,
        out_shape=out_shape,
        scratch_shapes=[pltpu.VMEM((1, HEAD_W), F32)],
        compiler_params=_params("arbitrary"),
        name="post",
    )(*([p] * N_SEC), pf, gqa, gka, gqb, gkb, gsgu, bf, slope, wm, bm, ltri)


def _lam_value(lam_ref, lam_init):
    a = lam_ref[...]
    s1 = jnp.sum(a[0:1] * a[1:2], axis=-1, keepdims=True)
    s2 = jnp.sum(a[2:3] * a[3:4], axis=-1, keepdims=True)
    return jnp.exp(s1) - jnp.exp(s2) + lam_init


def _merge_maps(outs, lam_ref, gsub_ref, lam_init):
    if len(outs) == 1:
        return outs[0]
    o = outs[0] - _lam_value(lam_ref, lam_init) * outs[1]
    return _rms(o, gsub_ref[...]) * (1.0 - lam_init)


FLASH_BLOCK = 1024
FLASH_LANES = 512


def _flash_kernel(qi_tab, ki_tab, qt_ref, k_ref, vt_ref, lam_ref, gsub_ref, o_ref,
                  m_scr, acc_scr, *, n_maps, lam_init):
    s = pl.program_id(1)
    qi = qi_tab[s]
    ki = ki_tab[s]

    @pl.when(ki == 0)
    def _():
        m_scr[...] = jnp.full_like(m_scr, NEG_INF)
        acc_scr[...] = jnp.zeros_like(acc_scr)

    def step(diagonal):
        k = k_ref[...]
        vt = vt_ref[...]
        tb = k.shape[0]
        tq = min(FLASH_LANES, tb)
        work = [(mp, c) for mp in range(n_maps) for c in range(tb // tq)]

        def scores(mp, c):
            return _dot(k, qt_ref[mp, :, c * tq:(c + 1) * tq])

        st_next = scores(*work[0])
        for i, (mp, c) in enumerate(work):
            st = st_next
            if i + 1 < len(work):
                st_next = scores(*work[i + 1])
            cs = slice(c * tq, (c + 1) * tq)
            if diagonal:
                key = lax.broadcasted_iota(jnp.int32, st.shape, 0)
                qry = lax.broadcasted_iota(jnp.int32, st.shape, 1) + c * tq
                st = jnp.where(key <= qry, st, NEG_INF)
            m_prev = m_scr[mp, :, cs]
            m_new = jnp.maximum(m_prev, jnp.max(st, axis=0, keepdims=True))
            p = jnp.exp2(st - m_new)
            acc_scr[mp, :, cs] = (jnp.exp2(m_prev - m_new) * acc_scr[mp, :, cs]
                                  + _dot(vt, p.astype(BF16)))
            m_scr[mp, :, cs] = m_new

    @pl.when(ki < qi)
    def _():
        step(False)

    @pl.when(ki == qi)
    def _():
        step(True)
        outs = []
        for mp in range(n_maps):
            acc = acc_scr[mp]
            outs.append((acc[:HEAD_W] / acc[HEAD_W:HEAD_W + 1]).T)
        o_ref[...] = _merge_maps(outs, lam_ref, gsub_ref, lam_init).astype(o_ref.dtype)


def _flash(qt, k, vt, lam_rows, gsub, n_rows, lam_init):
    n_maps = qt.shape[1]
    v_rows = vt.shape[1]
    tb = _pick_tile(n_rows, (FLASH_BLOCK, 512, 256, 128))
    nq = n_rows // tb
    qi_tab = np.concatenate([np.full(i + 1, i, np.int32) for i in range(nq)])
    ki_tab = np.concatenate([np.arange(i + 1, dtype=np.int32) for i in range(nq)])
    grid_spec = pltpu.PrefetchScalarGridSpec(
        num_scalar_prefetch=2,
        grid=(N_HEADS, len(qi_tab)),
        in_specs=[
            pl.BlockSpec((None, n_maps, HEAD_W + AUG_W, tb), lambda h, s, qt, kt: (h, 0, 0, qt[s])),
            pl.BlockSpec((None, tb, HEAD_W + AUG_W), lambda h, s, qt, kt: (h, kt[s], 0)),
            pl.BlockSpec((None, v_rows, tb), lambda h, s, qt, kt: (h, 0, kt[s])),
            pl.BlockSpec(lam_rows.shape, lambda h, s, qt, kt: (0, 0)),
            pl.BlockSpec(gsub.shape, lambda h, s, qt, kt: (0, 0)),
        ],
        out_specs=pl.BlockSpec((tb, HEAD_W), lambda h, s, qt, kt: (qt[s], h)),
        scratch_shapes=[
            pltpu.VMEM((n_maps, 1, tb), F32),
            pltpu.VMEM((n_maps, v_rows, tb), F32),
        ],
    )
    return pl.pallas_call(
        functools.partial(_flash_kernel, n_maps=n_maps, lam_init=lam_init),
        grid_spec=grid_spec,
        out_shape=jax.ShapeDtypeStruct((n_rows, SEC_W), BF16),
        compiler_params=_params("parallel", "arbitrary"),
        name=f"flash{n_maps}",
    )(jnp.asarray(qi_tab), jnp.asarray(ki_tab), qt, k, vt, lam_rows, gsub)


def _pagecum_kernel(x_ref, u_ref, ut_ref, wc_ref, wt_ref):
    hi, mid, lo = (a.astype(BF16) for a in _split3(x_ref[...]))
    u = u_ref[...]
    ut = ut_ref[...]
    wc_ref[...] = _dot(hi, u) + _dot(mid, u) + _dot(lo, u)
    wt_ref[...] = _dot(hi, ut) + _dot(mid, ut) + _dot(lo, ut)


def _pagecum(x, u, ut):
    r = x.shape[0]
    tr = _pick_tile(r, (512, 256, 128, 64, 32, 16, 8))
    spec = pl.BlockSpec((tr, PAGE_FLAT), lambda i: (i, 0))
    wspec = pl.BlockSpec((PAGE_FLAT, PAGE_FLAT), lambda i: (0, 0))
    return pl.pallas_call(
        _pagecum_kernel,
        grid=(r // tr,),
        in_specs=[spec, wspec, wspec],
        out_specs=[spec, spec],
        out_shape=[jax.ShapeDtypeStruct((r, PAGE_FLAT), F32)] * 2,
        compiler_params=_params("parallel"),
        name="pagecum",
    )(x, u, ut)


PAGES_PER_STEP = 8


def _softmax_update(scores, pv_fn, m_scr, l_scr, acc_scr):
    m_prev = m_scr[...]
    m_tile = functools.reduce(jnp.maximum, scores)
    m_new = jnp.maximum(m_prev, jnp.max(m_tile, axis=-1, keepdims=True))
    alpha = jnp.exp2(m_prev - m_new)
    ps = [jnp.exp2(sc - m_new) for sc in scores]
    row_sum = jnp.sum(ps[0], axis=-1, keepdims=True)
    for p in ps[1:]:
        row_sum = row_sum + jnp.sum(p, axis=-1, keepdims=True)
    l_scr[...] = alpha * l_scr[...] + row_sum
    acc_scr[...] = alpha * acc_scr[...] + pv_fn(ps)
    m_scr[...] = m_new


def _decode_forget_kernel(pt_ref, *refs, n_pages):
    pps = _pages_per_step(n_pages)
    q_ref = refs[0]
    k_refs = refs[1:1 + pps]
    v_refs = refs[1 + pps:1 + 2 * pps]
    wc_refs = refs[1 + 2 * pps:1 + 3 * pps]
    wt_refs = refs[1 + 3 * pps:1 + 4 * pps]
    cnew_ref, knew_ref, vnew_ref, o_ref, m_scr, l_scr, acc_scr, carry_scr = refs[1 + 4 * pps:]

    step = pl.program_id(1)
    n_steps = n_pages // pps
    q = q_ref[...]
    r = q.shape[0]

    @pl.when(step == 0)
    def _():
        m_scr[...] = jnp.full_like(m_scr, NEG_INF)
        l_scr[...] = jnp.zeros_like(l_scr)
        acc_scr[...] = jnp.zeros_like(acc_scr)
        carry_scr[...] = jnp.zeros_like(carry_scr)

    def scores(k, bias, causal):
        c = k.shape[0]
        sc = _dot_nt(q, k.astype(BF16)) + bias
        row = lax.broadcasted_iota(jnp.int32, (r, c), 0)
        col = lax.broadcasted_iota(jnp.int32, (r, c), 1)
        ok = (col & (N_HEADS - 1)) == ((row >> 3) & (N_HEADS - 1))
        if causal:
            ok = ok & ((col >> 3) <= (row & 7))
        return jnp.where(ok, sc, NEG_INF)

    carry = carry_scr[...]
    tiles = []
    for j in range(pps):
        tiles.append(scores(k_refs[j][...], (carry + wc_refs[j][...]) * (-LOG2E), False))
        carry = carry + wt_refs[j][...]
    carry_scr[...] = carry
    pv = lambda ps: sum(_dot(p.astype(BF16), v_refs[j][...].astype(BF16)) for j, p in enumerate(ps))
    _softmax_update(tiles, pv, m_scr, l_scr, acc_scr)

    @pl.when(step == n_steps - 1)
    def _():
        c = knew_ref.shape[0]
        bias = (carry_scr[:, :c] + cnew_ref[:, :c]) * (-LOG2E)
        tile = scores(knew_ref[...], bias, True)
        pv = lambda ps: _dot(ps[0].astype(BF16), vnew_ref[...].astype(BF16))
        _softmax_update([tile], pv, m_scr, l_scr, acc_scr)
        o_ref[...] = acc_scr[...] / l_scr[...]


def _decode_diff_kernel(pt_ref, *refs, n_pages, lam_init):
    pps = _pages_per_step(n_pages)
    q_ref = refs[0]
    kt_refs = refs[1:1 + pps]
    v_refs = refs[1 + pps:1 + 2 * pps]
    (slope_ref, knew_ref, vnew_ref, lam_ref, gsub_ref, o_ref,
     m_scr, l_scr, acc_scr, kpad_scr, vpad_scr) = refs[1 + 2 * pps:]

    step = pl.program_id(1)
    n_steps = n_pages // pps
    q = q_ref[...]
    r = q.shape[0]
    rows_per_head = r // N_HEADS
    n_new = knew_ref.shape[0]

    @pl.when(step == 0)
    def _():
        m_scr[...] = jnp.full_like(m_scr, NEG_INF)
        l_scr[...] = jnp.zeros_like(l_scr)
        acc_scr[...] = jnp.zeros_like(acc_scr)

    tok = lax.broadcasted_iota(jnp.int32, (1, PAGE), 1)

    def head_pv(ps, value_of):
        outs = []
        for h in range(N_HEADS):
            rs = slice(h * rows_per_head, (h + 1) * rows_per_head)
            outs.append(sum(_dot(p[rs].astype(BF16), value_of(j, h)) for j, p in enumerate(ps)))
        return jnp.concatenate(outs, axis=0)

    def page_scores(kt_ref):
        parts = []
        for c in range(SEC_W // MXU_DEPTH):
            rows = slice(c * r * MXU_DEPTH // SEC_W, (c + 1) * r * MXU_DEPTH // SEC_W)
            cols = slice(c * MXU_DEPTH, (c + 1) * MXU_DEPTH)
            parts.append(_dot(q_ref[rows, cols], kt_ref[cols, :].astype(BF16)))
        return jnp.concatenate(parts, axis=0)

    tiles = []
    for j in range(pps):
        pos = (tok + (step * pps + j) * PAGE).astype(F32)
        tiles.append(page_scores(kt_refs[j]) + slope_ref[...] * pos)
    page_value = lambda j, h: v_refs[j][pl.ds(h, PAGE, stride=N_HEADS), :].astype(BF16)
    _softmax_update(tiles, lambda ps: head_pv(ps, page_value), m_scr, l_scr, acc_scr)

    @pl.when(step == n_steps - 1)
    def _():
        kpad_scr[...] = jnp.zeros_like(kpad_scr)
        vpad_scr[...] = jnp.zeros_like(vpad_scr)
        kpad_scr[0:n_new, :] = knew_ref[...]
        vpad_scr[0:n_new, :] = vnew_ref[...]
        pos = (tok + n_pages * PAGE).astype(F32)
        sc = _dot_nt(q, kpad_scr[...].astype(BF16)) + slope_ref[...] * pos
        q_idx = lax.broadcasted_iota(jnp.int32, (r, PAGE), 0) & 7
        sc = jnp.where(tok <= q_idx, sc, NEG_INF)
        value_of = lambda j, h: vpad_scr[:, h * HEAD_W:(h + 1) * HEAD_W].astype(BF16)
        _softmax_update([sc], lambda ps: head_pv(ps, value_of), m_scr, l_scr, acc_scr)
        o = (acc_scr[...] / l_scr[...]).reshape(N_HEADS, 2, rows_per_head // 2, HEAD_W)
        outs = [o[:, mp].reshape(r // 2, HEAD_W) for mp in range(2)]
        o_ref[...] = _merge_maps(outs, lam_ref, gsub_ref, lam_init)


def _pages_per_step(n_pages):
    return _pick_tile(n_pages, (PAGES_PER_STEP, 4, 2, 1))


def _decode_specs(n_pages, layer):
    pps = _pages_per_step(n_pages)
    n_steps = n_pages // pps

    def cache_spec(j):
        return pl.BlockSpec((None, None, PAGE_FLAT, HEAD_W),
                            lambda bi, s, pt, j=j: (layer, pt[bi, s * pps + j], 0, 0))

    def cum_spec(j):
        return pl.BlockSpec((None, 1, PAGE_FLAT), lambda bi, s, pt, j=j: (pt[bi, s * pps + j], 0, 0))

    per_seq = lambda shape: pl.BlockSpec((None,) + shape, lambda bi, s, pt: (bi, 0, 0))
    whole2 = lambda a: pl.BlockSpec(a.shape, lambda bi, s, pt: (0, 0))
    return n_steps, cache_spec, cum_spec, per_seq, whole2


def _decode_forget(page_table, q, k_cache, v_cache, layer, knew, vnew, wc, wt, cnew):
    b, r, _ = q.shape
    n_pages = page_table.shape[1]
    pps = _pages_per_step(n_pages)
    n_steps, cache_spec, cum_spec, per_seq, _ = _decode_specs(n_pages, layer)
    c_new = knew.shape[1]
    pages = lambda spec: [spec(j) for j in range(pps)]
    in_specs = ([per_seq((r, HEAD_W))] + pages(cache_spec) + pages(cache_spec) + pages(cum_spec)
                + pages(cum_spec) + [per_seq((1, PAGE_FLAT)), per_seq((c_new, HEAD_W)),
                                     per_seq((c_new, HEAD_W))])
    args = [q] + [k_cache] * pps + [v_cache] * pps + [wc] * pps + [wt] * pps + [cnew, knew, vnew]
    grid_spec = pltpu.PrefetchScalarGridSpec(
        num_scalar_prefetch=1,
        grid=(b, n_steps),
        in_specs=in_specs,
        out_specs=per_seq((r, HEAD_W)),
        scratch_shapes=[
            pltpu.VMEM((r, 1), F32),
            pltpu.VMEM((r, 1), F32),
            pltpu.VMEM((r, HEAD_W), F32),
            pltpu.VMEM((1, PAGE_FLAT), F32),
        ],
    )
    return pl.pallas_call(
        functools.partial(_decode_forget_kernel, n_pages=n_pages),
        grid_spec=grid_spec,
        out_shape=jax.ShapeDtypeStruct((b, r, HEAD_W), F32),
        compiler_params=_params("parallel", "arbitrary"),
        name="decode_forget",
    )(page_table, *args)


def _decode_diff(page_table, q, kt_cache, v_cache, layer, knew, vnew, slope_rows, lam_rows, gsub,
                 lam_init):
    b, r, _ = q.shape
    n_pages = page_table.shape[1]
    pps = _pages_per_step(n_pages)
    n_steps, cache_spec, _, per_seq, whole2 = _decode_specs(n_pages, layer)
    n_new = knew.shape[1]
    pages = lambda spec: [spec(j) for j in range(pps)]
    in_specs = ([per_seq((r, SEC_W))] + pages(cache_spec) + pages(cache_spec)
                + [whole2(slope_rows), per_seq((n_new, SEC_W)), per_seq((n_new, SEC_W)),
                   whole2(lam_rows), whole2(gsub)])
    args = [q] + [kt_cache] * pps + [v_cache] * pps + [slope_rows, knew, vnew, lam_rows, gsub]
    grid_spec = pltpu.PrefetchScalarGridSpec(
        num_scalar_prefetch=1,
        grid=(b, n_steps),
        in_specs=in_specs,
        out_specs=per_seq((r // 2, HEAD_W)),
        scratch_shapes=[
            pltpu.VMEM((r, 1), F32),
            pltpu.VMEM((r, 1), F32),
            pltpu.VMEM((r, HEAD_W), F32),
            pltpu.VMEM((PAGE, SEC_W), F32),
            pltpu.VMEM((PAGE, SEC_W), F32),
        ],
    )
    return pl.pallas_call(
        functools.partial(_decode_diff_kernel, n_pages=n_pages, lam_init=lam_init),
        grid_spec=grid_spec,
        out_shape=jax.ShapeDtypeStruct((b, r // 2, HEAD_W), F32),
        compiler_params=_params("parallel", "arbitrary"),
        name="decode_diff",
    )(page_table, *args)


def _zmerge_kernel(h_ref, oa_ref, ob_ref, oc_ref, wa_ref, wb_ref, wc_ref,
                   wga_ref, wgb_ref, wgc_ref, bga_ref, bgb_ref, bgc_ref, z_ref):
    h = h_ref[...]
    ga = jax.nn.sigmoid(_dot(h, wga_ref[...]) + bga_ref[...])
    gb = jax.nn.sigmoid(_dot(h, wgb_ref[...]) + bgb_ref[...])
    gc = jax.nn.sigmoid(_dot(h, wgc_ref[...]) + bgc_ref[...])
    z = (ga * _dot(oa_ref[...], wa_ref[...]) + gb * _dot(ob_ref[...], wb_ref[...])
         + gc * _dot(oc_ref[...], wc_ref[...]))
    z_ref[...] = z.astype(z_ref.dtype)


def _zmerge(h, oa, ob, oc, wa, wb, wc, wg, bg, layer):
    m, d = h.shape
    tm = _pick_tile(m, (768, 512, 256))
    tn = _pick_tile(d, (512, 256))
    nn = d // tn
    rows = lambda w: pl.BlockSpec((tm, w), lambda i, j: (i, 0))
    wcol = lambda kdim, off: pl.BlockSpec((None, kdim, tn), lambda i, j, off=off: (layer, 0, j + off))
    bcol = lambda off: pl.BlockSpec((None, 1, tn), lambda i, j, off=off: (layer, 0, j + off))
    return pl.pallas_call(
        _zmerge_kernel,
        grid=(m // tm, nn),
        in_specs=[rows(d), rows(SEC_W), rows(SEC_W), rows(SEC_W),
                  wcol(SEC_W, 0), wcol(SEC_W, 0), wcol(SEC_W, 0),
                  wcol(d, 0), wcol(d, nn), wcol(d, 2 * nn),
                  bcol(0), bcol(nn), bcol(2 * nn)],
        out_specs=pl.BlockSpec((tm, tn), lambda i, j: (i, j)),
        out_shape=jax.ShapeDtypeStruct((m, d), BF16),
        compiler_params=_params("parallel", "arbitrary"),
        name="zmerge",
    )(h, oa, ob, oc, wa, wb, wc, wg, wg, wg, bg, bg, bg)


def _outproj_kernel(x_ref, z_ref, w_ref, o_ref):
    o_ref[...] = x_ref[...] + _dot(z_ref[...], w_ref[...])


def _outproj(x, z, w, layer):
    m, d = x.shape
    tm = _pick_tile(m, (768, 512, 256))
    tn = _pick_tile(d, (1024, 512))
    return pl.pallas_call(
        _outproj_kernel,
        grid=(m // tm, d // tn),
        in_specs=[pl.BlockSpec((tm, tn), lambda i, j: (i, j)),
                  pl.BlockSpec((tm, d), lambda i, j: (i, 0)),
                  pl.BlockSpec((None, d, tn), lambda i, j: (layer, 0, j))],
        out_specs=pl.BlockSpec((tm, tn), lambda i, j: (i, j)),
        out_shape=jax.ShapeDtypeStruct((m, d), F32),
        compiler_params=_params("parallel", "arbitrary"),
        name="outproj",
    )(x, z, w)


def _ffn_kernel(x_ref, g_ref, wg_ref, wu_ref, wd_ref, o_ref, h_scr):
    @pl.when(pl.program_id(1) == 0)
    def _():
        x = x_ref[...]
        h_scr[...] = _rms(x, g_ref[...]).astype(BF16)
        o_ref[...] = x

    h = h_scr[...]
    a = jax.nn.silu(_dot(h, wg_ref[...])) * _dot(h, wu_ref[...])
    o_ref[...] += _dot(a.astype(BF16), wd_ref[...])


def _ffn(x, g, wg, wu, wd, layer):
    m, d = x.shape
    f = wg.shape[2]
    tm = _pick_tile(m, (768, 512, 256))
    tf = _pick_tile(f, (512, 256))
    return pl.pallas_call(
        _ffn_kernel,
        grid=(m // tm, f // tf),
        in_specs=[pl.BlockSpec((tm, d), lambda i, j: (i, 0)),
                  pl.BlockSpec((1, d), lambda i, j: (0, 0)),
                  pl.BlockSpec((None, d, tf), lambda i, j: (layer, 0, j)),
                  pl.BlockSpec((None, d, tf), lambda i, j: (layer, 0, j)),
                  pl.BlockSpec((None, tf, d), lambda i, j: (layer, j, 0))],
        out_specs=pl.BlockSpec((tm, d), lambda i, j: (i, 0)),
        out_shape=jax.ShapeDtypeStruct((m, d), F32),
        scratch_shapes=[pltpu.VMEM((tm, d), BF16)],
        compiler_params=_params("parallel", "arbitrary"),
        name="ffn",
    )(x, g, wg, wu, wd)


def _page_cum_matrices():
    idx = np.arange(PAGE_FLAT)
    tok, head = idx >> 3, idx & 7
    same = head[:, None] == head[None, :]
    u = same & (tok[:, None] <= tok[None, :])
    return jnp.asarray(u, BF16), jnp.asarray(same, BF16)


def _mix_matrices(w_spatial, b_spatial, n_sample_tok):
    t = ROW_TILE
    tril = jnp.where(jnp.tril(jnp.ones((CHUNK, CHUNK), bool)), w_spatial, 0.0)
    eye_p = jnp.eye(t // CHUNK, dtype=F32)
    wm_p = jnp.einsum("ab,gts->gatbs", eye_p, tril).reshape(N_HEADS, t, t)
    bm_p = jnp.tile(b_spatial, (1, t // CHUNK))
    n = n_sample_tok
    eye_s = jnp.eye(t // n, dtype=F32)
    wm_s = jnp.einsum("ab,gts->gatbs", eye_s, tril[:, :n, :n]).reshape(N_HEADS, t, t)
    bm_s = jnp.tile(b_spatial[:, :n], (1, t // n))
    wm = jnp.stack([wm_p, wm_s]).astype(BF16)
    bm = jnp.stack([bm_p, bm_s])
    bm = jnp.pad(jnp.swapaxes(bm, 1, 2), ((0, 0), (0, 0), (0, HEAD_W - N_HEADS)))
    return wm, bm


def _lane_row(v, width=HEAD_W):
    v = v.reshape(1, -1)
    return jnp.pad(v, ((0, 0), (0, width - v.shape[1])))


def kernel(x_prompt, x_sample, cache_diff_k, cache_diff_v, cache_fox_k, cache_fox_v, cache_fox_logf,
           page_table, g_mix, w_in, b_forget, g_q_a, g_k_a, lam_q1, lam_k1, lam_q2, lam_k2, g_subln,
           g_q_b, g_k_b, g_sgu, w_spatial, b_spatial, w_branch_a, w_branch_b, w_branch_c, w_gate,
           b_gate, w_out, g_ffn, w_ffn_gate, w_ffn_up, w_ffn_down):
    depth = g_mix.shape[0]
    n_batch, seq, d_model = x_prompt.shape
    dec_b, dec_t, _ = x_sample.shape
    n_phys = cache_diff_k.shape[1]
    n_dec = dec_b * dec_t
    assert n_batch == 1 and d_model == N_HEADS * 2 * HEAD_W
    assert seq % ROW_TILE == 0 and n_dec == ROW_TILE and dec_t == 8
    assert cache_diff_k.shape[2] == PAGE
    m = seq + n_dec

    x = jnp.concatenate([x_prompt.reshape(seq, d_model), x_sample.reshape(n_dec, d_model)], axis=0)

    cdk = jnp.transpose(cache_diff_k, (0, 1, 3, 4, 5, 2)).reshape(depth, n_phys, PAGE_FLAT, PAGE)
    cdv = cache_diff_v.reshape(depth, n_phys, PAGE_FLAT, HEAD_W)
    cfk = cache_fox_k.reshape(depth, n_phys, PAGE_FLAT, HEAD_W)
    cfv = cache_fox_v.reshape(depth, n_phys, PAGE_FLAT, HEAD_W)
    cfl = cache_fox_logf.reshape(depth, n_phys, PAGE_FLAT)

    slopes = np.exp2(-8.0 * np.arange(1, N_HEADS + 1) / N_HEADS).astype(np.float32)
    slope_row = jnp.asarray(np.pad(slopes, (0, HEAD_W - N_HEADS)).reshape(1, HEAD_W))
    slope_rows = jnp.asarray(np.broadcast_to(np.repeat(slopes * np.float32(LOG2E), 2 * dec_t)[:, None],
                                             (2 * N_HEADS * dec_t, PAGE)))
    head_block = jnp.asarray((np.arange(SEC_W) // HEAD_W)[None, :] == np.arange(N_HEADS)[:, None]
                             ).reshape(1, N_HEADS, 1, 1, SEC_W)
    ltri = jnp.asarray(np.tril(np.ones((ROW_TILE, ROW_TILE), np.float32)), BF16)
    u_cum, u_tot = _page_cum_matrices()
    widths = np.cumsum([SEC_W] * 6 + [N_HEADS])

    wa16, wb16, wc16 = w_branch_a.astype(BF16), w_branch_b.astype(BF16), w_branch_c.astype(BF16)
    wg16, wo16 = w_gate.astype(BF16), w_out.astype(BF16)
    wfg16, wfu16, wfd16 = w_ffn_gate.astype(BF16), w_ffn_up.astype(BF16), w_ffn_down.astype(BF16)

    rows_prompt, rows_sample = [], []
    for layer in range(depth):
        lam_init = 0.8 - 0.6 * math.exp(-0.3 * layer)
        w_l = w_in[layer]
        w_main = jnp.concatenate([w_l[:, :widths[5]], w_l[:, widths[6]:]], axis=1).astype(BF16)
        w_f = jnp.pad(w_l[:, widths[5]:widths[6]], ((0, 0), (0, AUG_W - N_HEADS))).astype(BF16)
        p, pf, h = _proj_in(x, g_mix[layer].reshape(1, d_model), w_main, w_f)

        wm, bm = _mix_matrices(w_spatial[layer], b_spatial[layer], dec_t)
        lam_rows = jnp.pad(jnp.stack([lam_q1[layer], lam_k1[layer], lam_q2[layer], lam_k2[layer]]),
                           ((0, 4), (0, HEAD_W - MAP_W)))
        gsub = g_subln[layer].reshape(1, HEAD_W)
        (kan, kbn, logf, vcn, oc, qa_o, ka_o, va_o, qb_o, kb_o, vb_o) = _post(
            p, pf,
            jnp.tile(g_q_a[layer], 2).reshape(1, HEAD_W), jnp.tile(g_k_a[layer], 2).reshape(1, HEAD_W),
            g_q_b[layer].reshape(1, HEAD_W), g_k_b[layer].reshape(1, HEAD_W),
            g_sgu[layer].reshape(1, SEC_W), _lane_row(b_forget[layer]), slope_row,
            wm, bm, ltri, seq // ROW_TILE)
        va = p[:, 2 * SEC_W:3 * SEC_W]
        vb = p[:, 5 * SEC_W:6 * SEC_W]

        oa_p = _flash(qa_o, ka_o, va_o, lam_rows, gsub, seq, lam_init)
        ob_p = _flash(qb_o, kb_o, vb_o, lam_rows, gsub, seq, lam_init)

        to_rows = lambda a: a[seq:].reshape(dec_b, dec_t * N_HEADS, HEAD_W)
        per_seq = lambda a: a[seq:].reshape(dec_b, dec_t, SEC_W)
        q_hm = jnp.transpose(qa_o[:, :, :HEAD_W, seq:].reshape(N_HEADS, 2, HEAD_W, dec_b, dec_t),
                             (3, 0, 1, 4, 2))
        q_a = jnp.where(head_block, jnp.tile(q_hm, (1, 1, 1, 1, N_HEADS)), 0).reshape(
            dec_b, 2 * N_HEADS * dec_t, SEC_W)
        q_b = jnp.transpose(qb_o[:, 0, :HEAD_W, seq:].reshape(N_HEADS, HEAD_W, dec_b, dec_t),
                            (2, 0, 3, 1)).reshape(dec_b, N_HEADS * dec_t, HEAD_W)
        oa_s = _decode_diff(page_table, q_a, cdk, cdv, layer, per_seq(kan), per_seq(va), slope_rows,
                            lam_rows, gsub, lam_init)
        wc, wt = _pagecum(cfl[layer], u_cum, u_tot)
        lf_new = jnp.pad(logf[seq:, :N_HEADS].reshape(dec_b, dec_t * N_HEADS),
                         ((0, 0), (0, PAGE_FLAT - dec_t * N_HEADS)))
        cnew, _ = _pagecum(lf_new, u_cum, u_tot)
        ob_s = _decode_forget(page_table, q_b, cfk, cfv, layer, to_rows(kbn), to_rows(vb),
                              wc.reshape(n_phys, 1, PAGE_FLAT), wt.reshape(n_phys, 1, PAGE_FLAT),
                              cnew.reshape(dec_b, 1, PAGE_FLAT))
        from_rows = lambda o: jnp.transpose(o.reshape(dec_b, N_HEADS, dec_t, HEAD_W),
                                            (0, 2, 1, 3)).reshape(n_dec, SEC_W).astype(BF16)
        oa = jnp.concatenate([oa_p, from_rows(oa_s)], axis=0)
        ob = jnp.concatenate([ob_p, from_rows(ob_s)], axis=0)

        z = _zmerge(h, oa, ob, oc, wa16, wb16, wc16, wg16, b_gate.reshape(depth, 1, -1), layer)
        x = _outproj(x, z, wo16, layer)
        x = _ffn(x, g_ffn[layer].reshape(1, d_model), wfg16, wfu16, wfd16, layer)

        lf = logf[:, :N_HEADS]
        rows_prompt.append((kan[:seq], va[:seq], kbn[:seq], vb[:seq], lf[:seq]))
        rows_sample.append((kan[seq:], va[seq:], kbn[seq:], vb[seq:], lf[seq:], vcn[seq:]))

    stack = lambda rows, i, shape: jnp.stack([r[i] for r in rows], axis=0).reshape((depth,) + shape)
    p_shape, s_shape = (n_batch, seq), (dec_b, dec_t)
    return (
        x[:seq].reshape(n_batch, seq, d_model),
        x[seq:].reshape(dec_b, dec_t, d_model),
        stack(rows_prompt, 0, p_shape + (N_HEADS, 2, MAP_W)),
        stack(rows_prompt, 1, p_shape + (N_HEADS, HEAD_W)),
        stack(rows_prompt, 2, p_shape + (N_HEADS, HEAD_W)),
        stack(rows_prompt, 3, p_shape + (N_HEADS, HEAD_W)),
        stack(rows_prompt, 4, p_shape + (N_HEADS,)),
        stack(rows_sample, 0, s_shape + (N_HEADS, 2, MAP_W)),
        stack(rows_sample, 1, s_shape + (N_HEADS, HEAD_W)),
        stack(rows_sample, 2, s_shape + (N_HEADS, HEAD_W)),
        stack(rows_sample, 3, s_shape + (N_HEADS, HEAD_W)),
        stack(rows_sample, 4, s_shape + (N_HEADS,)),
        stack(rows_sample, 5, s_shape + (SEC_W,)),
    )
```

```python
import functools
import math

import numpy as np
import jax
import jax.numpy as jnp
from jax import lax
from jax.experimental import pallas as pl
from jax.experimental.pallas import tpu as pltpu

F32 = jnp.float32
BF16 = jnp.bfloat16

N_HEADS = 8
HEAD_W = 128
MAP_W = 64
SEC_W = N_HEADS * HEAD_W
N_SEC = 8
CHUNK = 128
PAGE = 128
PAGE_FLAT = PAGE * N_HEADS
AUG_W = 128
ONES_ROWS = 16
LOG2E = 1.4426950408889634
EPS = 1e-6
NEG_INF = float("-inf")

MXU_DEPTH = 256
ROW_TILE = MXU_DEPTH
VMEM_LIMIT = 56 * 1024 * 1024


def _params(*sem):
    return pltpu.CompilerParams(dimension_semantics=sem, vmem_limit_bytes=VMEM_LIMIT)


def _pick_tile(n, candidates):
    for c in candidates:
        if n % c == 0:
            return c
    raise ValueError(f"no tile in {candidates} divides {n}")


def _rms(x, g):
    ms = jnp.mean(x * x, axis=-1, keepdims=True)
    return x * lax.rsqrt(ms + EPS) * g


def _split3(x):
    hi = x.astype(BF16).astype(F32)
    r = x - hi
    mid = r.astype(BF16).astype(F32)
    lo = (r - mid).astype(BF16).astype(F32)
    return hi, mid, lo


def _gelu(x):
    return 0.5 * x * (1.0 + lax.erf(x * (2.0 ** -0.5)))


def _dot(a, b):
    return jnp.dot(a, b, preferred_element_type=F32)


def _dot_nt(a, b):
    return lax.dot_general(a, b, (((1,), (1,)), ((), ())), preferred_element_type=F32)


def _proj_in_kernel(x_ref, g_ref, w_ref, wf_ref, p_ref, pf_ref, h_ref):
    @pl.when(pl.program_id(1) == 0)
    def _():
        hb = _rms(x_ref[...], g_ref[...]).astype(BF16)
        h_ref[...] = hb
        pf_ref[...] = _dot(hb, wf_ref[...])

    p_ref[...] = _dot(h_ref[...], w_ref[...])


def _proj_in(x, g, w_main, w_f):
    m, d = x.shape
    n = w_main.shape[1]
    tm = _pick_tile(m, (768, 512, 256))
    tn = _pick_tile(n, (1024, 512))
    return pl.pallas_call(
        _proj_in_kernel,
        grid=(m // tm, n // tn),
        in_specs=[
            pl.BlockSpec((tm, d), lambda i, j: (i, 0)),
            pl.BlockSpec((1, d), lambda i, j: (0, 0)),
            pl.BlockSpec((d, tn), lambda i, j: (0, j)),
            pl.BlockSpec((d, AUG_W), lambda i, j: (0, 0)),
        ],
        out_specs=[
            pl.BlockSpec((tm, tn), lambda i, j: (i, j)),
            pl.BlockSpec((tm, AUG_W), lambda i, j: (i, 0)),
            pl.BlockSpec((tm, d), lambda i, j: (i, 0)),
        ],
        out_shape=[
            jax.ShapeDtypeStruct((m, n), F32),
            jax.ShapeDtypeStruct((m, AUG_W), F32),
            jax.ShapeDtypeStruct((m, d), BF16),
        ],
        compiler_params=_params("parallel", "arbitrary"),
        name="proj_in",
    )(x, g, w_main, w_f)


def _norm_halves(xh, g):
    lo = lax.broadcasted_iota(jnp.int32, xh.shape, 1) < MAP_W
    sq = xh * xh
    s_lo = jnp.sum(jnp.where(lo, sq, 0.0), axis=-1, keepdims=True)
    s_hi = jnp.sum(jnp.where(lo, 0.0, sq), axis=-1, keepdims=True)
    ms = jnp.where(lo, s_lo, s_hi) * (1.0 / MAP_W)
    return xh * lax.rsqrt(ms + EPS) * g


N_POST_IN = 19
N_CACHE_ROWS = 4


def _post_kernel(*refs, n_prompt_tiles):
    (qa_ref, ka_ref, va_ref, qb_ref, kb_ref, vb_ref, uc_ref, vc_ref, pf_ref,
     gqa_ref, gka_ref, gqb_ref, gkb_ref, gsgu_ref, bf_ref, slope_ref,
     wm_ref, bm_ref, ltri_ref) = refs[:N_POST_IN]
    (kan_ref, kbn_ref, logf_ref, vcn_ref, oc_ref, qa_o, ka_o, va_o, qb_o, kb_o, vb_o,
     kan_pt, va_p, kbn_p, vb_p, carry_ref, kant_scr) = refs[-(13 + N_CACHE_ROWS):]
    i = pl.program_id(0)
    t = qa_ref.shape[0]
    lane = lax.broadcasted_iota(jnp.int32, (t, HEAD_W), 1)
    row = lax.broadcasted_iota(jnp.int32, (t, HEAD_W), 0) + i * t

    @pl.when(i == 0)
    def _():
        carry_ref[...] = jnp.zeros_like(carry_ref)

    logf = jnp.where(lane < N_HEADS, jax.nn.log_sigmoid(pf_ref[...] + bf_ref[...]), 0.0)
    logf_ref[...] = logf
    ltri = ltri_ref[...]
    hi, mid, lo = _split3(logf)
    cum = (_dot(ltri, hi.astype(BF16)) + _dot(ltri, mid.astype(BF16)) + _dot(ltri, lo.astype(BF16))
           + carry_ref[...])
    carry_ref[...] = cum[t - 1:t, :]

    gl = _gelu(vc_ref[...])
    vcn = _rms(gl, gsgu_ref[...])
    vcn_ref[...] = vcn
    bm = bm_ref[...]

    feat = lax.broadcasted_iota(jnp.int32, (HEAD_W, t), 0)
    lo_feat = feat < MAP_W
    ones_q = jnp.where(feat < 3, 1.0, 0.0)
    ones_v = jnp.ones((ONES_ROWS, t), F32)
    pos = row.astype(F32)

    def key_term(x):
        hi, mid, lo = _split3(x)
        return jnp.where(lane == 0, hi, jnp.where(lane == 1, mid, jnp.where(lane == 2, lo, 0.0)))

    for h in range(N_HEADS):
        cs = slice(h * HEAD_W, (h + 1) * HEAD_W)
        mixed = _dot(wm_ref[h], vcn[:, cs].astype(BF16)) + bm[:, h:h + 1]
        oc_ref[:, cs] = (_gelu(uc_ref[:, cs]) * mixed).astype(BF16)

        qt = (_norm_halves(qa_ref[:, cs], gqa_ref[...]) * (MAP_W ** -0.5 * LOG2E)).T
        kn = _norm_halves(ka_ref[:, cs], gka_ref[...])
        kan_ref[:, cs] = kn
        kant_scr[cs, :] = kn.T
        qa_o[h, 0] = jnp.concatenate([jnp.where(lo_feat, qt, 0.0), ones_q], axis=0).astype(BF16)
        qa_o[h, 1] = jnp.concatenate([jnp.where(lo_feat, 0.0, qt), ones_q], axis=0).astype(BF16)
        aug = key_term(pos[:, :1] * (slope_ref[:, h:h + 1] * LOG2E))
        ka_o[h] = jnp.concatenate([kn, aug], axis=-1).astype(BF16)
        va_o[h] = jnp.concatenate([va_ref[:, cs].T, ones_v], axis=0).astype(BF16)

        qt = (_rms(qb_ref[:, cs], gqb_ref[...]) * (HEAD_W ** -0.5 * LOG2E)).T
        kn = _rms(kb_ref[:, cs], gkb_ref[...])
        kbn_ref[:, cs] = kn
        qb_o[h, 0] = jnp.concatenate([qt, ones_q], axis=0).astype(BF16)
        aug = key_term(cum[:, h:h + 1] * (-LOG2E))
        kb_o[h] = jnp.concatenate([kn, aug], axis=-1).astype(BF16)
        vb_o[h] = jnp.concatenate([vb_ref[:, cs].T, ones_v], axis=0).astype(BF16)

    @pl.when(i < n_prompt_tiles)
    def _():
        kan_pt[...] = kant_scr[...]
        va_p[...] = va_ref[...]
        kbn_p[...] = kbn_ref[...]
        vb_p[...] = vb_ref[...]


def _post(p, pf, gqa, gka, gqb, gkb, gsgu, bf, slope, wm, bm, ltri, n_prompt_tiles, depth, layer,
          cache_rows):
    m = p.shape[0]
    t = ROW_TILE
    nt = m // t
    assert len(cache_rows) in (0, N_CACHE_ROWS)

    def sec(k):
        return pl.BlockSpec((t, SEC_W), lambda i, k=k: (i, k))

    def whole(a):
        nd = a.ndim
        return pl.BlockSpec(a.shape, lambda i: (0,) * nd)

    def sel(i):
        return jnp.where(i < n_prompt_tiles, 0, 1)

    row_f32 = lambda w: pl.BlockSpec((t, w), lambda i: (i, 0))
    out_shape = [
        jax.ShapeDtypeStruct((m, SEC_W), F32),
        jax.ShapeDtypeStruct((m, SEC_W), F32),
        jax.ShapeDtypeStruct((m, HEAD_W), F32),
        jax.ShapeDtypeStruct((m, SEC_W), F32),
        jax.ShapeDtypeStruct((m, SEC_W), BF16),
        jax.ShapeDtypeStruct((N_HEADS, 2, HEAD_W + AUG_W, m), BF16),
        jax.ShapeDtypeStruct((N_HEADS, m, HEAD_W + AUG_W), BF16),
        jax.ShapeDtypeStruct((N_HEADS, HEAD_W + ONES_ROWS, m), BF16),
        jax.ShapeDtypeStruct((N_HEADS, 1, HEAD_W + AUG_W, m), BF16),
        jax.ShapeDtypeStruct((N_HEADS, m, HEAD_W + AUG_W), BF16),
        jax.ShapeDtypeStruct((N_HEADS, HEAD_W + ONES_ROWS, m), BF16),
    ]
    qt_spec = lambda n: pl.BlockSpec((N_HEADS, n, HEAD_W + AUG_W, t), lambda i: (0, 0, 0, i))
    k_spec = pl.BlockSpec((N_HEADS, t, HEAD_W + AUG_W), lambda i: (0, i, 0))
    vt_spec = pl.BlockSpec((N_HEADS, HEAD_W + ONES_ROWS, t), lambda i: (0, 0, i))
    stacked_spec = pl.BlockSpec((None, t, SEC_W),
                                lambda i: (layer, jnp.minimum(i, n_prompt_tiles - 1), 0))
    out_specs = [
        row_f32(SEC_W), row_f32(SEC_W), row_f32(HEAD_W), row_f32(SEC_W), row_f32(SEC_W),
        qt_spec(2), k_spec, vt_spec, qt_spec(1), k_spec, vt_spec,
    ] + [pl.BlockSpec((None, SEC_W, t), lambda i: (layer, 0, jnp.minimum(i, n_prompt_tiles - 1)))
         ] + [stacked_spec] * (N_CACHE_ROWS - 1)
    out_shape += [jax.ShapeDtypeStruct((depth, SEC_W, n_prompt_tiles * t), F32)]
    out_shape += [jax.ShapeDtypeStruct((depth, n_prompt_tiles * t, SEC_W), F32)] * (N_CACHE_ROWS - 1)
    n_out_base = len(out_shape) - N_CACHE_ROWS
    in_specs = [sec(k) for k in range(N_SEC)] + [
        row_f32(AUG_W),
        whole(gqa), whole(gka), whole(gqb), whole(gkb), whole(gsgu), whole(bf), whole(slope),
        pl.BlockSpec((None, N_HEADS, t, t), lambda i: (sel(i), 0, 0, 0)),
        pl.BlockSpec((None, t, HEAD_W), lambda i: (sel(i), 0, 0)),
        whole(ltri),
    ] + [pl.BlockSpec(memory_space=pl.ANY)] * len(cache_rows)
    assert len(in_specs) == N_POST_IN + len(cache_rows)
    return pl.pallas_call(
        functools.partial(_post_kernel, n_prompt_tiles=n_prompt_tiles),
        grid=(nt,),
        in_specs=in_specs,
        out_specs=out_specs,
        out_shape=out_shape,
        input_output_aliases={N_POST_IN + k: n_out_base + k for k in range(len(cache_rows))},
        scratch_shapes=[pltpu.VMEM((1, HEAD_W), F32), pltpu.VMEM((SEC_W, t), F32)],
        compiler_params=_params("arbitrary"),
        name="post",
    )(*([p] * N_SEC), pf, gqa, gka, gqb, gkb, gsgu, bf, slope, wm, bm, ltri, *cache_rows)


def _lam_value(lam_ref, lam_init):
    a = lam_ref[...]
    s1 = jnp.sum(a[0:1] * a[1:2], axis=-1, keepdims=True)
    s2 = jnp.sum(a[2:3] * a[3:4], axis=-1, keepdims=True)
    return jnp.exp(s1) - jnp.exp(s2) + lam_init


def _merge_maps(outs, lam_ref, gsub_ref, lam_init):
    if len(outs) == 1:
        return outs[0]
    o = outs[0] - _lam_value(lam_ref, lam_init) * outs[1]
    return _rms(o, gsub_ref[...]) * (1.0 - lam_init)


FLASH_BLOCK = 2048
FLASH_KEYS = 1024
FLASH_LANES = 512


def _flash_kernel(qi_tab, ki_tab, qt_ref, k_ref, vt_ref, lam_ref, gsub_ref, o_ref,
                  m_scr, acc_scr, *, n_maps, lam_init):
    s = pl.program_id(1)
    qi = qi_tab[s]
    ki = ki_tab[s]

    @pl.when(ki == 0)
    def _():
        m_scr[...] = jnp.full_like(m_scr, NEG_INF)
        acc_scr[...] = jnp.zeros_like(acc_scr)

    def step(diagonal):
        tb = k_ref.shape[0]
        tq = min(FLASH_LANES, tb)
        tk = min(FLASH_KEYS, tb)
        work = []
        for mp in range(n_maps):
            for c in range(tb // tq):
                k_end = (c + 1) * tq if diagonal else tb
                for k_lo in range(0, k_end, tk):
                    work.append((mp, c, k_lo, min(k_lo + tk, k_end)))

        def scores(mp, c, k_lo, k_hi):
            return _dot(k_ref[k_lo:k_hi, :], qt_ref[mp, :, c * tq:(c + 1) * tq])

        st_next = scores(*work[0])
        for i, (mp, c, k_lo, k_hi) in enumerate(work):
            st = st_next
            if i + 1 < len(work):
                st_next = scores(*work[i + 1])
            cs = slice(c * tq, (c + 1) * tq)
            vt = vt_ref[:, k_lo:k_hi]
            if diagonal and k_hi > c * tq + 1:
                key = lax.broadcasted_iota(jnp.int32, st.shape, 0) + k_lo
                qry = lax.broadcasted_iota(jnp.int32, st.shape, 1) + c * tq
                st = jnp.where(key <= qry, st, NEG_INF)
            m_prev = m_scr[mp, :, cs]
            m_new = jnp.maximum(m_prev, jnp.max(st, axis=0, keepdims=True))
            p = jnp.exp2(st - m_new)
            acc_scr[mp, :, cs] = (jnp.exp2(m_prev - m_new) * acc_scr[mp, :, cs]
                                  + _dot(vt, p.astype(BF16)))
            m_scr[mp, :, cs] = m_new

    @pl.when(ki < qi)
    def _():
        step(False)

    @pl.when(ki == qi)
    def _():
        step(True)
        outs = []
        for mp in range(n_maps):
            acc = acc_scr[mp]
            outs.append((acc[:HEAD_W] / acc[HEAD_W:HEAD_W + 1]).T)
        o_ref[...] = _merge_maps(outs, lam_ref, gsub_ref, lam_init).astype(o_ref.dtype)


def _flash(qt, k, vt, lam_rows, gsub, n_rows, lam_init):
    n_maps = qt.shape[1]
    v_rows = vt.shape[1]
    tb = _pick_tile(n_rows, (FLASH_BLOCK, 512, 256, 128))
    nq = n_rows // tb
    qi_tab = np.concatenate([np.full(i + 1, i, np.int32) for i in range(nq)])
    ki_tab = np.concatenate([np.arange(i + 1, dtype=np.int32) for i in range(nq)])
    grid_spec = pltpu.PrefetchScalarGridSpec(
        num_scalar_prefetch=2,
        grid=(N_HEADS, len(qi_tab)),
        in_specs=[
            pl.BlockSpec((None, n_maps, HEAD_W + AUG_W, tb), lambda h, s, qt, kt: (h, 0, 0, qt[s])),
            pl.BlockSpec((None, tb, HEAD_W + AUG_W), lambda h, s, qt, kt: (h, kt[s], 0)),
            pl.BlockSpec((None, v_rows, tb), lambda h, s, qt, kt: (h, 0, kt[s])),
            pl.BlockSpec(lam_rows.shape, lambda h, s, qt, kt: (0, 0)),
            pl.BlockSpec(gsub.shape, lambda h, s, qt, kt: (0, 0)),
        ],
        out_specs=pl.BlockSpec((tb, HEAD_W), lambda h, s, qt, kt: (qt[s], h)),
        scratch_shapes=[
            pltpu.VMEM((n_maps, 1, tb), F32),
            pltpu.VMEM((n_maps, v_rows, tb), F32),
        ],
    )
    return pl.pallas_call(
        functools.partial(_flash_kernel, n_maps=n_maps, lam_init=lam_init),
        grid_spec=grid_spec,
        out_shape=jax.ShapeDtypeStruct((n_rows, SEC_W), BF16),
        compiler_params=_params("parallel", "arbitrary"),
        name=f"flash{n_maps}",
    )(jnp.asarray(qi_tab), jnp.asarray(ki_tab), qt, k, vt, lam_rows, gsub)


def _pagecum_kernel(x_ref, u_ref, ut_ref, wc_ref, wt_ref):
    hi, mid, lo = (a.astype(BF16) for a in _split3(x_ref[...]))
    u = u_ref[...]
    ut = ut_ref[...]
    wc_ref[...] = _dot(hi, u) + _dot(mid, u) + _dot(lo, u)
    wt_ref[...] = _dot(hi, ut) + _dot(mid, ut) + _dot(lo, ut)


def _pagecum(x, u, ut):
    r = x.shape[0]
    tr = _pick_tile(r, (512, 256, 128, 64, 32, 16, 8))
    spec = pl.BlockSpec((tr, PAGE_FLAT), lambda i: (i, 0))
    wspec = pl.BlockSpec((PAGE_FLAT, PAGE_FLAT), lambda i: (0, 0))
    return pl.pallas_call(
        _pagecum_kernel,
        grid=(r // tr,),
        in_specs=[spec, wspec, wspec],
        out_specs=[spec, spec],
        out_shape=[jax.ShapeDtypeStruct((r, PAGE_FLAT), F32)] * 2,
        compiler_params=_params("parallel"),
        name="pagecum",
    )(x, u, ut)


PAGES_PER_STEP = 16


def _softmax_update(scores, pv_fn, m_scr, l_scr, acc_scr):
    m_prev = m_scr[...]
    m_tile = functools.reduce(jnp.maximum, scores)
    m_new = jnp.maximum(m_prev, jnp.max(m_tile, axis=-1, keepdims=True))
    alpha = jnp.exp2(m_prev - m_new)
    ps = [jnp.exp2(sc - m_new) for sc in scores]
    row_sum = jnp.sum(ps[0], axis=-1, keepdims=True)
    for p in ps[1:]:
        row_sum = row_sum + jnp.sum(p, axis=-1, keepdims=True)
    l_scr[...] = alpha * l_scr[...] + row_sum
    acc_scr[...] = alpha * acc_scr[...] + pv_fn(ps)
    m_scr[...] = m_new


def _decode_forget_kernel(pt_ref, *refs, n_pages):
    pps = _pages_per_step(n_pages)
    q_ref = refs[0]
    k_refs = refs[1:1 + pps]
    v_refs = refs[1 + pps:1 + 2 * pps]
    wc_refs = refs[1 + 2 * pps:1 + 3 * pps]
    wt_refs = refs[1 + 3 * pps:1 + 4 * pps]
    cnew_ref, knew_ref, vnew_ref, o_ref, m_scr, l_scr, acc_scr, carry_scr = refs[1 + 4 * pps:]

    step = pl.program_id(1)
    n_steps = n_pages // pps
    q = q_ref[...]
    r = q.shape[0]

    @pl.when(step == 0)
    def _():
        m_scr[...] = jnp.full_like(m_scr, NEG_INF)
        l_scr[...] = jnp.zeros_like(l_scr)
        acc_scr[...] = jnp.zeros_like(acc_scr)
        carry_scr[...] = jnp.zeros_like(carry_scr)

    def scores(k, bias, causal):
        c = k.shape[0]
        sc = _dot_nt(q, k.astype(BF16)) + bias
        row = lax.broadcasted_iota(jnp.int32, (r, c), 0)
        col = lax.broadcasted_iota(jnp.int32, (r, c), 1)
        ok = (col & (N_HEADS - 1)) == ((row >> 3) & (N_HEADS - 1))
        if causal:
            ok = ok & ((col >> 3) <= (row & 7))
        return jnp.where(ok, sc, NEG_INF)

    carry = carry_scr[...]
    tiles = []
    for j in range(pps):
        tiles.append(scores(k_refs[j][...], (carry + wc_refs[j][...]) * (-LOG2E), False))
        carry = carry + wt_refs[j][...]
    carry_scr[...] = carry
    pv = lambda ps: sum(_dot(p.astype(BF16), v_refs[j][...].astype(BF16)) for j, p in enumerate(ps))
    _softmax_update(tiles, pv, m_scr, l_scr, acc_scr)

    @pl.when(step == n_steps - 1)
    def _():
        c = knew_ref.shape[0]
        bias = (carry_scr[:, :c] + cnew_ref[:, :c]) * (-LOG2E)
        tile = scores(knew_ref[...], bias, True)
        pv = lambda ps: _dot(ps[0].astype(BF16), vnew_ref[...].astype(BF16))
        _softmax_update([tile], pv, m_scr, l_scr, acc_scr)
        o_ref[...] = acc_scr[...] / l_scr[...]


def _decode_diff_kernel(pt_ref, *refs, n_pages, lam_init):
    pps = _pages_per_step(n_pages)
    q_ref = refs[0]
    kt_refs = refs[1:1 + pps]
    v_refs = refs[1 + pps:1 + 2 * pps]
    (slope_ref, knew_ref, vnew_ref, lam_ref, gsub_ref, o_ref,
     m_scr, l_scr, acc_scr, kpad_scr, vpad_scr) = refs[1 + 2 * pps:]

    step = pl.program_id(1)
    n_steps = n_pages // pps
    q = q_ref[...]
    r = q.shape[0]
    rows_per_head = r // N_HEADS
    n_new = knew_ref.shape[0]

    @pl.when(step == 0)
    def _():
        m_scr[...] = jnp.full_like(m_scr, NEG_INF)
        l_scr[...] = jnp.zeros_like(l_scr)
        acc_scr[...] = jnp.zeros_like(acc_scr)

    tok = lax.broadcasted_iota(jnp.int32, (1, PAGE), 1)

    def head_pv(ps, value_of):
        outs = []
        for h in range(N_HEADS):
            rs = slice(h * rows_per_head, (h + 1) * rows_per_head)
            outs.append(sum(_dot(p[rs].astype(BF16), value_of(j, h)) for j, p in enumerate(ps)))
        return jnp.concatenate(outs, axis=0)

    def page_scores(kt_ref):
        parts = []
        for c in range(SEC_W // MXU_DEPTH):
            rows = slice(c * r * MXU_DEPTH // SEC_W, (c + 1) * r * MXU_DEPTH // SEC_W)
            cols = slice(c * MXU_DEPTH, (c + 1) * MXU_DEPTH)
            parts.append(_dot(q_ref[rows, cols], kt_ref[cols, :].astype(BF16)))
        return jnp.concatenate(parts, axis=0)

    tiles = []
    for j in range(pps):
        pos = (tok + (step * pps + j) * PAGE).astype(F32)
        tiles.append(page_scores(kt_refs[j]) + slope_ref[...] * pos)
    page_value = lambda j, h: v_refs[j][pl.ds(h, PAGE, stride=N_HEADS), :].astype(BF16)
    _softmax_update(tiles, lambda ps: head_pv(ps, page_value), m_scr, l_scr, acc_scr)

    @pl.when(step == n_steps - 1)
    def _():
        kpad_scr[...] = jnp.zeros_like(kpad_scr)
        vpad_scr[...] = jnp.zeros_like(vpad_scr)
        kpad_scr[0:n_new, :] = knew_ref[...]
        vpad_scr[0:n_new, :] = vnew_ref[...]
        pos = (tok + n_pages * PAGE).astype(F32)
        sc = _dot_nt(q, kpad_scr[...].astype(BF16)) + slope_ref[...] * pos
        q_idx = lax.broadcasted_iota(jnp.int32, (r, PAGE), 0) & 7
        sc = jnp.where(tok <= q_idx, sc, NEG_INF)
        value_of = lambda j, h: vpad_scr[:, h * HEAD_W:(h + 1) * HEAD_W].astype(BF16)
        _softmax_update([sc], lambda ps: head_pv(ps, value_of), m_scr, l_scr, acc_scr)
        o = (acc_scr[...] / l_scr[...]).reshape(N_HEADS, 2, rows_per_head // 2, HEAD_W)
        outs = [o[:, mp].reshape(r // 2, HEAD_W) for mp in range(2)]
        o_ref[...] = _merge_maps(outs, lam_ref, gsub_ref, lam_init)


def _pages_per_step(n_pages):
    return _pick_tile(n_pages, (PAGES_PER_STEP, 4, 2, 1))


def _decode_specs(n_pages, layer):
    pps = _pages_per_step(n_pages)
    n_steps = n_pages // pps

    def cache_spec(j):
        return pl.BlockSpec((None, None, PAGE_FLAT, HEAD_W),
                            lambda bi, s, pt, j=j: (layer, pt[bi, s * pps + j], 0, 0))

    def cum_spec(j):
        return pl.BlockSpec((None, 1, PAGE_FLAT), lambda bi, s, pt, j=j: (pt[bi, s * pps + j], 0, 0))

    per_seq = lambda shape: pl.BlockSpec((None,) + shape, lambda bi, s, pt: (bi, 0, 0))
    whole2 = lambda a: pl.BlockSpec(a.shape, lambda bi, s, pt: (0, 0))
    return n_steps, cache_spec, cum_spec, per_seq, whole2


def _decode_forget(page_table, q, k_cache, v_cache, layer, knew, vnew, wc, wt, cnew):
    b, r, _ = q.shape
    n_pages = page_table.shape[1]
    pps = _pages_per_step(n_pages)
    n_steps, cache_spec, cum_spec, per_seq, _ = _decode_specs(n_pages, layer)
    c_new = knew.shape[1]
    pages = lambda spec: [spec(j) for j in range(pps)]
    in_specs = ([per_seq((r, HEAD_W))] + pages(cache_spec) + pages(cache_spec) + pages(cum_spec)
                + pages(cum_spec) + [per_seq((1, PAGE_FLAT)), per_seq((c_new, HEAD_W)),
                                     per_seq((c_new, HEAD_W))])
    args = [q] + [k_cache] * pps + [v_cache] * pps + [wc] * pps + [wt] * pps + [cnew, knew, vnew]
    grid_spec = pltpu.PrefetchScalarGridSpec(
        num_scalar_prefetch=1,
        grid=(b, n_steps),
        in_specs=in_specs,
        out_specs=per_seq((r, HEAD_W)),
        scratch_shapes=[
            pltpu.VMEM((r, 1), F32),
            pltpu.VMEM((r, 1), F32),
            pltpu.VMEM((r, HEAD_W), F32),
            pltpu.VMEM((1, PAGE_FLAT), F32),
        ],
    )
    return pl.pallas_call(
        functools.partial(_decode_forget_kernel, n_pages=n_pages),
        grid_spec=grid_spec,
        out_shape=jax.ShapeDtypeStruct((b, r, HEAD_W), F32),
        compiler_params=_params("parallel", "arbitrary"),
        name="decode_forget",
    )(page_table, *args)


def _decode_diff(page_table, q, kt_cache, v_cache, layer, knew, vnew, slope_rows, lam_rows, gsub,
                 lam_init):
    b, r, _ = q.shape
    n_pages = page_table.shape[1]
    pps = _pages_per_step(n_pages)
    n_steps, cache_spec, _, per_seq, whole2 = _decode_specs(n_pages, layer)
    n_new = knew.shape[1]
    pages = lambda spec: [spec(j) for j in range(pps)]
    in_specs = ([per_seq((r, SEC_W))] + pages(cache_spec) + pages(cache_spec)
                + [whole2(slope_rows), per_seq((n_new, SEC_W)), per_seq((n_new, SEC_W)),
                   whole2(lam_rows), whole2(gsub)])
    args = [q] + [kt_cache] * pps + [v_cache] * pps + [slope_rows, knew, vnew, lam_rows, gsub]
    grid_spec = pltpu.PrefetchScalarGridSpec(
        num_scalar_prefetch=1,
        grid=(b, n_steps),
        in_specs=in_specs,
        out_specs=per_seq((r // 2, HEAD_W)),
        scratch_shapes=[
            pltpu.VMEM((r, 1), F32),
            pltpu.VMEM((r, 1), F32),
            pltpu.VMEM((r, HEAD_W), F32),
            pltpu.VMEM((PAGE, SEC_W), F32),
            pltpu.VMEM((PAGE, SEC_W), F32),
        ],
    )
    return pl.pallas_call(
        functools.partial(_decode_diff_kernel, n_pages=n_pages, lam_init=lam_init),
        grid_spec=grid_spec,
        out_shape=jax.ShapeDtypeStruct((b, r // 2, HEAD_W), F32),
        compiler_params=_params("parallel", "arbitrary"),
        name="decode_diff",
    )(page_table, *args)


def _zmerge_kernel(h_ref, oa_ref, ob_ref, oc_ref, wa_ref, wb_ref, wc_ref,
                   wga_ref, wgb_ref, wgc_ref, bga_ref, bgb_ref, bgc_ref, z_ref):
    h = h_ref[...]
    ga = jax.nn.sigmoid(_dot(h, wga_ref[...]) + bga_ref[...])
    gb = jax.nn.sigmoid(_dot(h, wgb_ref[...]) + bgb_ref[...])
    gc = jax.nn.sigmoid(_dot(h, wgc_ref[...]) + bgc_ref[...])
    z = (ga * _dot(oa_ref[...], wa_ref[...]) + gb * _dot(ob_ref[...], wb_ref[...])
         + gc * _dot(oc_ref[...], wc_ref[...]))
    z_ref[...] = z.astype(z_ref.dtype)


def _zmerge(h, oa, ob, oc, wa, wb, wc, wg, bg, layer):
    m, d = h.shape
    tm = _pick_tile(m, (768, 512, 256))
    tn = _pick_tile(d, (512, 256))
    nn = d // tn
    rows = lambda w: pl.BlockSpec((tm, w), lambda i, j: (i, 0))
    wcol = lambda kdim, off: pl.BlockSpec((None, kdim, tn), lambda i, j, off=off: (layer, 0, j + off))
    bcol = lambda off: pl.BlockSpec((None, 1, tn), lambda i, j, off=off: (layer, 0, j + off))
    return pl.pallas_call(
        _zmerge_kernel,
        grid=(m // tm, nn),
        in_specs=[rows(d), rows(SEC_W), rows(SEC_W), rows(SEC_W),
                  wcol(SEC_W, 0), wcol(SEC_W, 0), wcol(SEC_W, 0),
                  wcol(d, 0), wcol(d, nn), wcol(d, 2 * nn),
                  bcol(0), bcol(nn), bcol(2 * nn)],
        out_specs=pl.BlockSpec((tm, tn), lambda i, j: (i, j)),
        out_shape=jax.ShapeDtypeStruct((m, d), BF16),
        compiler_params=_params("parallel", "arbitrary"),
        name="zmerge",
    )(h, oa, ob, oc, wa, wb, wc, wg, wg, wg, bg, bg, bg)


def _outproj_kernel(x_ref, z_ref, w_ref, o_ref):
    o_ref[...] = x_ref[...] + _dot(z_ref[...], w_ref[...])


def _outproj(x, z, w, layer):
    m, d = x.shape
    tm = _pick_tile(m, (768, 512, 256))
    tn = _pick_tile(d, (1024, 512))
    return pl.pallas_call(
        _outproj_kernel,
        grid=(m // tm, d // tn),
        in_specs=[pl.BlockSpec((tm, tn), lambda i, j: (i, j)),
                  pl.BlockSpec((tm, d), lambda i, j: (i, 0)),
                  pl.BlockSpec((None, d, tn), lambda i, j: (layer, 0, j))],
        out_specs=pl.BlockSpec((tm, tn), lambda i, j: (i, j)),
        out_shape=jax.ShapeDtypeStruct((m, d), F32),
        compiler_params=_params("parallel", "arbitrary"),
        name="outproj",
    )(x, z, w)


def _ffn_kernel(x_ref, g_ref, wg_ref, wu_ref, wd_ref, o_ref, h_scr):
    @pl.when(pl.program_id(1) == 0)
    def _():
        x = x_ref[...]
        h_scr[...] = _rms(x, g_ref[...]).astype(BF16)
        o_ref[...] = x

    h = h_scr[...]
    a = jax.nn.silu(_dot(h, wg_ref[...])) * _dot(h, wu_ref[...])
    o_ref[...] += _dot(a.astype(BF16), wd_ref[...])


def _ffn(x, g, wg, wu, wd, layer):
    m, d = x.shape
    f = wg.shape[2]
    tm = _pick_tile(m, (768, 512, 256))
    tf = _pick_tile(f, (512, 256))
    return pl.pallas_call(
        _ffn_kernel,
        grid=(m // tm, f // tf),
        in_specs=[pl.BlockSpec((tm, d), lambda i, j: (i, 0)),
                  pl.BlockSpec((1, d), lambda i, j: (0, 0)),
                  pl.BlockSpec((None, d, tf), lambda i, j: (layer, 0, j)),
                  pl.BlockSpec((None, d, tf), lambda i, j: (layer, 0, j)),
                  pl.BlockSpec((None, tf, d), lambda i, j: (layer, j, 0))],
        out_specs=pl.BlockSpec((tm, d), lambda i, j: (i, 0)),
        out_shape=jax.ShapeDtypeStruct((m, d), F32),
        scratch_shapes=[pltpu.VMEM((tm, d), BF16)],
        compiler_params=_params("parallel", "arbitrary"),
        name="ffn",
    )(x, g, wg, wu, wd)


def _page_cum_matrices():
    idx = np.arange(PAGE_FLAT)
    tok, head = idx >> 3, idx & 7
    same = head[:, None] == head[None, :]
    u = same & (tok[:, None] <= tok[None, :])
    return jnp.asarray(u, BF16), jnp.asarray(same, BF16)


def _mix_matrices(w_spatial, b_spatial, n_sample_tok):
    t = ROW_TILE
    tril = jnp.where(jnp.tril(jnp.ones((CHUNK, CHUNK), bool)), w_spatial, 0.0)
    eye_p = jnp.eye(t // CHUNK, dtype=F32)
    wm_p = jnp.einsum("ab,gts->gatbs", eye_p, tril).reshape(N_HEADS, t, t)
    bm_p = jnp.tile(b_spatial, (1, t // CHUNK))
    n = n_sample_tok
    eye_s = jnp.eye(t // n, dtype=F32)
    wm_s = jnp.einsum("ab,gts->gatbs", eye_s, tril[:, :n, :n]).reshape(N_HEADS, t, t)
    bm_s = jnp.tile(b_spatial[:, :n], (1, t // n))
    wm = jnp.stack([wm_p, wm_s]).astype(BF16)
    bm = jnp.stack([bm_p, bm_s])
    bm = jnp.pad(jnp.swapaxes(bm, 1, 2), ((0, 0), (0, 0), (0, HEAD_W - N_HEADS)))
    return wm, bm


def _lane_row(v, width=HEAD_W):
    v = v.reshape(1, -1)
    return jnp.pad(v, ((0, 0), (0, width - v.shape[1])))


def kernel(x_prompt, x_sample, cache_diff_k, cache_diff_v, cache_fox_k, cache_fox_v, cache_fox_logf,
           page_table, g_mix, w_in, b_forget, g_q_a, g_k_a, lam_q1, lam_k1, lam_q2, lam_k2, g_subln,
           g_q_b, g_k_b, g_sgu, w_spatial, b_spatial, w_branch_a, w_branch_b, w_branch_c, w_gate,
           b_gate, w_out, g_ffn, w_ffn_gate, w_ffn_up, w_ffn_down):
    depth = g_mix.shape[0]
    n_batch, seq, d_model = x_prompt.shape
    dec_b, dec_t, _ = x_sample.shape
    n_phys = cache_diff_k.shape[1]
    n_dec = dec_b * dec_t
    assert n_batch == 1 and d_model == N_HEADS * 2 * HEAD_W
    assert seq % ROW_TILE == 0 and n_dec == ROW_TILE and dec_t == 8
    assert cache_diff_k.shape[2] == PAGE
    m = seq + n_dec

    x = jnp.concatenate([x_prompt.reshape(seq, d_model), x_sample.reshape(n_dec, d_model)], axis=0)

    cdk = jnp.transpose(cache_diff_k, (0, 1, 3, 4, 5, 2)).reshape(depth, n_phys, PAGE_FLAT, PAGE)
    cdv = cache_diff_v.reshape(depth, n_phys, PAGE_FLAT, HEAD_W)
    cfk = cache_fox_k.reshape(depth, n_phys, PAGE_FLAT, HEAD_W)
    cfv = cache_fox_v.reshape(depth, n_phys, PAGE_FLAT, HEAD_W)
    cfl = cache_fox_logf.reshape(depth, n_phys, PAGE_FLAT)

    slopes = np.exp2(-8.0 * np.arange(1, N_HEADS + 1) / N_HEADS).astype(np.float32)
    slope_row = jnp.asarray(np.pad(slopes, (0, HEAD_W - N_HEADS)).reshape(1, HEAD_W))
    slope_rows = jnp.asarray(np.broadcast_to(np.repeat(slopes * np.float32(LOG2E), 2 * dec_t)[:, None],
                                             (2 * N_HEADS * dec_t, PAGE)))
    head_block = jnp.asarray((np.arange(SEC_W) // HEAD_W)[None, :] == np.arange(N_HEADS)[:, None]
                             ).reshape(1, N_HEADS, 1, 1, SEC_W)
    ltri = jnp.asarray(np.tril(np.ones((ROW_TILE, ROW_TILE), np.float32)), BF16)
    u_cum, u_tot = _page_cum_matrices()
    widths = np.cumsum([SEC_W] * 6 + [N_HEADS])

    wa16, wb16, wc16 = w_branch_a.astype(BF16), w_branch_b.astype(BF16), w_branch_c.astype(BF16)
    wg16, wo16 = w_gate.astype(BF16), w_out.astype(BF16)
    wfg16, wfu16, wfd16 = w_ffn_gate.astype(BF16), w_ffn_up.astype(BF16), w_ffn_down.astype(BF16)

    rows_prompt, rows_sample = [], []
    cache_rows = ()
    for layer in range(depth):
        lam_init = 0.8 - 0.6 * math.exp(-0.3 * layer)
        w_l = w_in[layer]
        w_main = jnp.concatenate([w_l[:, :widths[5]], w_l[:, widths[6]:]], axis=1).astype(BF16)
        w_f = jnp.pad(w_l[:, widths[5]:widths[6]], ((0, 0), (0, AUG_W - N_HEADS))).astype(BF16)
        p, pf, h = _proj_in(x, g_mix[layer].reshape(1, d_model), w_main, w_f)

        wm, bm = _mix_matrices(w_spatial[layer], b_spatial[layer], dec_t)
        lam_rows = jnp.pad(jnp.stack([lam_q1[layer], lam_k1[layer], lam_q2[layer], lam_k2[layer]]),
                           ((0, 4), (0, HEAD_W - MAP_W)))
        gsub = g_subln[layer].reshape(1, HEAD_W)
        (kan, kbn, logf, vcn, oc, qa_o, ka_o, va_o, qb_o, kb_o, vb_o, *cache_rows) = _post(
            p, pf,
            jnp.tile(g_q_a[layer], 2).reshape(1, HEAD_W), jnp.tile(g_k_a[layer], 2).reshape(1, HEAD_W),
            g_q_b[layer].reshape(1, HEAD_W), g_k_b[layer].reshape(1, HEAD_W),
            g_sgu[layer].reshape(1, SEC_W), _lane_row(b_forget[layer]), slope_row,
            wm, bm, ltri, seq // ROW_TILE, depth, layer, cache_rows)
        va = p[seq:, 2 * SEC_W:3 * SEC_W]
        vb = p[seq:, 5 * SEC_W:6 * SEC_W]

        oa_p = _flash(qa_o, ka_o, va_o, lam_rows, gsub, seq, lam_init)
        ob_p = _flash(qb_o, kb_o, vb_o, lam_rows, gsub, seq, lam_init)

        kan_s, kbn_s = kan[seq:], kbn[seq:]
        to_rows = lambda a: a.reshape(dec_b, dec_t * N_HEADS, HEAD_W)
        per_seq = lambda a: a.reshape(dec_b, dec_t, SEC_W)
        q_hm = jnp.transpose(qa_o[:, :, :HEAD_W, seq:].reshape(N_HEADS, 2, HEAD_W, dec_b, dec_t),
                             (3, 0, 1, 4, 2))
        q_a = jnp.where(head_block, jnp.tile(q_hm, (1, 1, 1, 1, N_HEADS)), 0).reshape(
            dec_b, 2 * N_HEADS * dec_t, SEC_W)
        q_b = jnp.transpose(qb_o[:, 0, :HEAD_W, seq:].reshape(N_HEADS, HEAD_W, dec_b, dec_t),
                            (2, 0, 3, 1)).reshape(dec_b, N_HEADS * dec_t, HEAD_W)
        oa_s = _decode_diff(page_table, q_a, cdk, cdv, layer, per_seq(kan_s), per_seq(va), slope_rows,
                            lam_rows, gsub, lam_init)
        wc, wt = _pagecum(cfl[layer], u_cum, u_tot)
        lf_new = jnp.pad(logf[seq:, :N_HEADS].reshape(dec_b, dec_t * N_HEADS),
                         ((0, 0), (0, PAGE_FLAT - dec_t * N_HEADS)))
        cnew, _ = _pagecum(lf_new, u_cum, u_tot)
        ob_s = _decode_forget(page_table, q_b, cfk, cfv, layer, to_rows(kbn_s), to_rows(vb),
                              wc.reshape(n_phys, 1, PAGE_FLAT), wt.reshape(n_phys, 1, PAGE_FLAT),
                              cnew.reshape(dec_b, 1, PAGE_FLAT))
        from_rows = lambda o: jnp.transpose(o.reshape(dec_b, N_HEADS, dec_t, HEAD_W),
                                            (0, 2, 1, 3)).reshape(n_dec, SEC_W).astype(BF16)
        oa = jnp.concatenate([oa_p, from_rows(oa_s)], axis=0)
        ob = jnp.concatenate([ob_p, from_rows(ob_s)], axis=0)

        z = _zmerge(h, oa, ob, oc, wa16, wb16, wc16, wg16, b_gate.reshape(depth, 1, -1), layer)
        x = _outproj(x, z, wo16, layer)
        x = _ffn(x, g_ffn[layer].reshape(1, d_model), wfg16, wfu16, wfd16, layer)

        lf = logf[:, :N_HEADS]
        rows_prompt.append((lf[:seq],))
        rows_sample.append((kan_s, va, kbn_s, vb, lf[seq:], vcn[seq:]))

    stack = lambda rows, i, shape: jnp.stack([r[i] for r in rows], axis=0).reshape((depth,) + shape)
    p_shape, s_shape = (n_batch, seq), (dec_b, dec_t)
    kan_pt, va_p, kbn_p, vb_p = cache_rows
    kan_p = jnp.transpose(kan_pt, (0, 2, 1))
    return (
        x[:seq].reshape(n_batch, seq, d_model),
        x[seq:].reshape(dec_b, dec_t, d_model),
        kan_p.reshape((depth,) + p_shape + (N_HEADS, 2, MAP_W)),
        va_p.reshape((depth,) + p_shape + (N_HEADS, HEAD_W)),
        kbn_p.reshape((depth,) + p_shape + (N_HEADS, HEAD_W)),
        vb_p.reshape((depth,) + p_shape + (N_HEADS, HEAD_W)),
        stack(rows_prompt, 0, p_shape + (N_HEADS,)),
        stack(rows_sample, 0, s_shape + (N_HEADS, 2, MAP_W)),
        stack(rows_sample, 1, s_shape + (N_HEADS, HEAD_W)),
        stack(rows_sample, 2, s_shape + (N_HEADS, HEAD_W)),
        stack(rows_sample, 3, s_shape + (N_HEADS, HEAD_W)),
        stack(rows_sample, 4, s_shape + (N_HEADS,)),
        stack(rows_sample, 5, s_shape + (SEC_W,)),
    )
```

```python
import functools
import math

import numpy as np
import jax
import jax.numpy as jnp
from jax import lax
from jax.experimental import pallas as pl
from jax.experimental.pallas import tpu as pltpu

F32 = jnp.float32
BF16 = jnp.bfloat16

N_HEADS = 8
HEAD_W = 128
MAP_W = 64
SEC_W = N_HEADS * HEAD_W
N_SEC = 8
CHUNK = 128
PAGE = 128
PAGE_FLAT = PAGE * N_HEADS
AUG_W = 128
ONES_ROWS = 16
LOG2E = 1.4426950408889634
EPS = 1e-6
NEG_INF = float("-inf")

MXU_DEPTH = 256
ROW_TILE = MXU_DEPTH
VMEM_LIMIT = 56 * 1024 * 1024


def _params(*sem):
    return pltpu.CompilerParams(dimension_semantics=sem, vmem_limit_bytes=VMEM_LIMIT)


def _pick_tile(n, candidates):
    for c in candidates:
        if n % c == 0:
            return c
    raise ValueError(f"no tile in {candidates} divides {n}")


def _rms(x, g):
    ms = jnp.mean(x * x, axis=-1, keepdims=True)
    return x * lax.rsqrt(ms + EPS) * g


def _split3(x):
    hi = x.astype(BF16).astype(F32)
    r = x - hi
    mid = r.astype(BF16).astype(F32)
    lo = (r - mid).astype(BF16).astype(F32)
    return hi, mid, lo


def _gelu(x):
    return 0.5 * x * (1.0 + lax.erf(x * (2.0 ** -0.5)))


def _dot(a, b):
    return jnp.dot(a, b, preferred_element_type=F32)


def _dot_nt(a, b):
    return lax.dot_general(a, b, (((1,), (1,)), ((), ())), preferred_element_type=F32)


def _proj_in_kernel(x_ref, g_ref, w_ref, wf_ref, p_ref, pf_ref, h_ref):
    @pl.when(pl.program_id(1) == 0)
    def _():
        hb = _rms(x_ref[...], g_ref[...]).astype(BF16)
        h_ref[...] = hb
        pf_ref[...] = _dot(hb, wf_ref[...])

    p_ref[...] = _dot(h_ref[...], w_ref[...])


def _proj_in(x, g, w_main, w_f):
    m, d = x.shape
    n = w_main.shape[1]
    tm = _pick_tile(m, (768, 512, 256))
    tn = _pick_tile(n, (1024, 512))
    return pl.pallas_call(
        _proj_in_kernel,
        grid=(m // tm, n // tn),
        in_specs=[
            pl.BlockSpec((tm, d), lambda i, j: (i, 0)),
            pl.BlockSpec((1, d), lambda i, j: (0, 0)),
            pl.BlockSpec((d, tn), lambda i, j: (0, j)),
            pl.BlockSpec((d, AUG_W), lambda i, j: (0, 0)),
        ],
        out_specs=[
            pl.BlockSpec((tm, tn), lambda i, j: (i, j)),
            pl.BlockSpec((tm, AUG_W), lambda i, j: (i, 0)),
            pl.BlockSpec((tm, d), lambda i, j: (i, 0)),
        ],
        out_shape=[
            jax.ShapeDtypeStruct((m, n), F32),
            jax.ShapeDtypeStruct((m, AUG_W), F32),
            jax.ShapeDtypeStruct((m, d), BF16),
        ],
        compiler_params=_params("parallel", "arbitrary"),
        name="proj_in",
    )(x, g, w_main, w_f)


def _norm_halves(xh, g):
    lo = lax.broadcasted_iota(jnp.int32, xh.shape, 1) < MAP_W
    sq = xh * xh
    s_lo = jnp.sum(jnp.where(lo, sq, 0.0), axis=-1, keepdims=True)
    s_hi = jnp.sum(jnp.where(lo, 0.0, sq), axis=-1, keepdims=True)
    ms = jnp.where(lo, s_lo, s_hi) * (1.0 / MAP_W)
    return xh * lax.rsqrt(ms + EPS) * g


N_POST_IN = 19
N_CACHE_ROWS = 4


def _post_kernel(*refs, n_prompt_tiles):
    (qa_ref, ka_ref, va_ref, qb_ref, kb_ref, vb_ref, uc_ref, vc_ref, pf_ref,
     gqa_ref, gka_ref, gqb_ref, gkb_ref, gsgu_ref, bf_ref, slope_ref,
     wm_ref, bm_ref, ltri_ref) = refs[:N_POST_IN]
    (kan_ref, kbn_ref, logf_ref, vcn_ref, oc_ref, qa_o, ka_o, va_o, qb_o, kb_o, vb_o,
     kan_pt, va_p, kbn_p, vb_p, carry_ref, kant_scr) = refs[-(13 + N_CACHE_ROWS):]
    i = pl.program_id(0)
    t = qa_ref.shape[0]
    lane = lax.broadcasted_iota(jnp.int32, (t, HEAD_W), 1)
    row = lax.broadcasted_iota(jnp.int32, (t, HEAD_W), 0) + i * t

    @pl.when(i == 0)
    def _():
        carry_ref[...] = jnp.zeros_like(carry_ref)

    logf = jnp.where(lane < N_HEADS, jax.nn.log_sigmoid(pf_ref[...] + bf_ref[...]), 0.0)
    logf_ref[...] = logf
    ltri = ltri_ref[...]
    hi, mid, lo = _split3(logf)
    cum = (_dot(ltri, hi.astype(BF16)) + _dot(ltri, mid.astype(BF16)) + _dot(ltri, lo.astype(BF16))
           + carry_ref[...])
    carry_ref[...] = cum[t - 1:t, :]

    gl = _gelu(vc_ref[...])
    vcn = _rms(gl, gsgu_ref[...])
    vcn_ref[...] = vcn
    bm = bm_ref[...]

    feat = lax.broadcasted_iota(jnp.int32, (HEAD_W, t), 0)
    lo_feat = feat < MAP_W
    ones_q = jnp.where(feat < 3, 1.0, 0.0)
    ones_v = jnp.ones((ONES_ROWS, t), F32)
    pos = row.astype(F32)

    def key_term(x):
        hi, mid, lo = _split3(x)
        return jnp.where(lane == 0, hi, jnp.where(lane == 1, mid, jnp.where(lane == 2, lo, 0.0)))

    for h in range(N_HEADS):
        cs = slice(h * HEAD_W, (h + 1) * HEAD_W)
        mixed = _dot(wm_ref[h], vcn[:, cs].astype(BF16)) + bm[:, h:h + 1]
        oc_ref[:, cs] = (_gelu(uc_ref[:, cs]) * mixed).astype(BF16)

        qt = (_norm_halves(qa_ref[:, cs], gqa_ref[...]) * (MAP_W ** -0.5 * LOG2E)).T
        kn = _norm_halves(ka_ref[:, cs], gka_ref[...])
        kan_ref[:, cs] = kn
        kant_scr[cs, :] = kn.T
        qa_o[h, 0] = jnp.concatenate([jnp.where(lo_feat, qt, 0.0), ones_q], axis=0).astype(BF16)
        qa_o[h, 1] = jnp.concatenate([jnp.where(lo_feat, 0.0, qt), ones_q], axis=0).astype(BF16)
        aug = key_term(pos[:, :1] * (slope_ref[:, h:h + 1] * LOG2E))
        ka_o[h] = jnp.concatenate([kn, aug], axis=-1).astype(BF16)
        va_o[h] = jnp.concatenate([va_ref[:, cs].T, ones_v], axis=0).astype(BF16)

        qt = (_rms(qb_ref[:, cs], gqb_ref[...]) * (HEAD_W ** -0.5 * LOG2E)).T
        kn = _rms(kb_ref[:, cs], gkb_ref[...])
        kbn_ref[:, cs] = kn
        qb_o[h, 0] = jnp.concatenate([qt, ones_q], axis=0).astype(BF16)
        aug = key_term(cum[:, h:h + 1] * (-LOG2E))
        kb_o[h] = jnp.concatenate([kn, aug], axis=-1).astype(BF16)
        vb_o[h] = jnp.concatenate([vb_ref[:, cs].T, ones_v], axis=0).astype(BF16)

    @pl.when(i < n_prompt_tiles)
    def _():
        kan_pt[...] = kant_scr[...]
        va_p[...] = va_ref[...]
        kbn_p[...] = kbn_ref[...]
        vb_p[...] = vb_ref[...]


def _post(p, pf, gqa, gka, gqb, gkb, gsgu, bf, slope, wm, bm, ltri, n_prompt_tiles, depth, layer,
          cache_rows):
    m = p.shape[0]
    t = ROW_TILE
    nt = m // t
    assert len(cache_rows) in (0, N_CACHE_ROWS)

    def sec(k):
        return pl.BlockSpec((t, SEC_W), lambda i, k=k: (i, k))

    def whole(a):
        nd = a.ndim
        return pl.BlockSpec(a.shape, lambda i: (0,) * nd)

    def sel(i):
        return jnp.where(i < n_prompt_tiles, 0, 1)

    row_f32 = lambda w: pl.BlockSpec((t, w), lambda i: (i, 0))
    out_shape = [
        jax.ShapeDtypeStruct((m, SEC_W), F32),
        jax.ShapeDtypeStruct((m, SEC_W), F32),
        jax.ShapeDtypeStruct((m, HEAD_W), F32),
        jax.ShapeDtypeStruct((m, SEC_W), F32),
        jax.ShapeDtypeStruct((m, SEC_W), BF16),
        jax.ShapeDtypeStruct((N_HEADS, 2, HEAD_W + AUG_W, m), BF16),
        jax.ShapeDtypeStruct((N_HEADS, m, HEAD_W + AUG_W), BF16),
        jax.ShapeDtypeStruct((N_HEADS, HEAD_W + ONES_ROWS, m), BF16),
        jax.ShapeDtypeStruct((N_HEADS, 1, HEAD_W + AUG_W, m), BF16),
        jax.ShapeDtypeStruct((N_HEADS, m, HEAD_W + AUG_W), BF16),
        jax.ShapeDtypeStruct((N_HEADS, HEAD_W + ONES_ROWS, m), BF16),
    ]
    qt_spec = lambda n: pl.BlockSpec((N_HEADS, n, HEAD_W + AUG_W, t), lambda i: (0, 0, 0, i))
    k_spec = pl.BlockSpec((N_HEADS, t, HEAD_W + AUG_W), lambda i: (0, i, 0))
    vt_spec = pl.BlockSpec((N_HEADS, HEAD_W + ONES_ROWS, t), lambda i: (0, 0, i))
    stacked_spec = pl.BlockSpec((None, t, SEC_W),
                                lambda i: (layer, jnp.minimum(i, n_prompt_tiles - 1), 0))
    out_specs = [
        row_f32(SEC_W), row_f32(SEC_W), row_f32(HEAD_W), row_f32(SEC_W), row_f32(SEC_W),
        qt_spec(2), k_spec, vt_spec, qt_spec(1), k_spec, vt_spec,
    ] + [pl.BlockSpec((None, SEC_W, t), lambda i: (layer, 0, jnp.minimum(i, n_prompt_tiles - 1)))
         ] + [stacked_spec] * (N_CACHE_ROWS - 1)
    out_shape += [jax.ShapeDtypeStruct((depth, SEC_W, n_prompt_tiles * t), F32)]
    out_shape += [jax.ShapeDtypeStruct((depth, n_prompt_tiles * t, SEC_W), F32)] * (N_CACHE_ROWS - 1)
    n_out_base = len(out_shape) - N_CACHE_ROWS
    in_specs = [sec(k) for k in range(N_SEC)] + [
        row_f32(AUG_W),
        whole(gqa), whole(gka), whole(gqb), whole(gkb), whole(gsgu), whole(bf), whole(slope),
        pl.BlockSpec((None, N_HEADS, t, t), lambda i: (sel(i), 0, 0, 0)),
        pl.BlockSpec((None, t, HEAD_W), lambda i: (sel(i), 0, 0)),
        whole(ltri),
    ] + [pl.BlockSpec(memory_space=pl.ANY)] * len(cache_rows)
    assert len(in_specs) == N_POST_IN + len(cache_rows)
    return pl.pallas_call(
        functools.partial(_post_kernel, n_prompt_tiles=n_prompt_tiles),
        grid=(nt,),
        in_specs=in_specs,
        out_specs=out_specs,
        out_shape=out_shape,
        input_output_aliases={N_POST_IN + k: n_out_base + k for k in range(len(cache_rows))},
        scratch_shapes=[pltpu.VMEM((1, HEAD_W), F32), pltpu.VMEM((SEC_W, t), F32)],
        compiler_params=_params("arbitrary"),
        name="post",
    )(*([p] * N_SEC), pf, gqa, gka, gqb, gkb, gsgu, bf, slope, wm, bm, ltri, *cache_rows)


def _lam_value(lam_ref, lam_init):
    a = lam_ref[...]
    s1 = jnp.sum(a[0:1] * a[1:2], axis=-1, keepdims=True)
    s2 = jnp.sum(a[2:3] * a[3:4], axis=-1, keepdims=True)
    return jnp.exp(s1) - jnp.exp(s2) + lam_init


def _merge_maps(outs, lam_ref, gsub_ref, lam_init):
    if len(outs) == 1:
        return outs[0]
    o = outs[0] - _lam_value(lam_ref, lam_init) * outs[1]
    return _rms(o, gsub_ref[...]) * (1.0 - lam_init)


FLASH_BLOCK = 2048
FLASH_KEYS = 2048
FLASH_LANES = 512


def _flash_kernel(qi_tab, ki_tab, qt_ref, k_ref, vt_ref, lam_ref, gsub_ref, o_ref,
                  m_scr, acc_scr, *, n_maps, lam_init):
    s = pl.program_id(1)
    qi = qi_tab[s]
    ki = ki_tab[s]

    @pl.when(ki == 0)
    def _():
        m_scr[...] = jnp.full_like(m_scr, NEG_INF)
        acc_scr[...] = jnp.zeros_like(acc_scr)

    def step(diagonal):
        tb = k_ref.shape[0]
        tq = min(FLASH_LANES, tb)
        tk = min(FLASH_KEYS, tb)
        work = []
        for mp in range(n_maps):
            for c in range(tb // tq):
                k_end = (c + 1) * tq if diagonal else tb
                for k_lo in range(0, k_end, tk):
                    work.append((mp, c, k_lo, min(k_lo + tk, k_end)))

        def scores(mp, c, k_lo, k_hi):
            return _dot(k_ref[k_lo:k_hi, :], qt_ref[mp, :, c * tq:(c + 1) * tq])

        st_next = scores(*work[0])
        for i, (mp, c, k_lo, k_hi) in enumerate(work):
            st = st_next
            if i + 1 < len(work):
                st_next = scores(*work[i + 1])
            cs = slice(c * tq, (c + 1) * tq)
            vt = vt_ref[:, k_lo:k_hi]
            if diagonal and k_hi > c * tq + 1:
                key = lax.broadcasted_iota(jnp.int32, st.shape, 0) + k_lo
                qry = lax.broadcasted_iota(jnp.int32, st.shape, 1) + c * tq
                st = jnp.where(key <= qry, st, NEG_INF)
            m_prev = m_scr[mp, :, cs]
            m_new = jnp.maximum(m_prev, jnp.max(st, axis=0, keepdims=True))
            p = jnp.exp2(st - m_new)
            acc_scr[mp, :, cs] = (jnp.exp2(m_prev - m_new) * acc_scr[mp, :, cs]
                                  + _dot(vt, p.astype(BF16)))
            m_scr[mp, :, cs] = m_new

    @pl.when(ki < qi)
    def _():
        step(False)

    @pl.when(ki == qi)
    def _():
        step(True)
        outs = []
        for mp in range(n_maps):
            acc = acc_scr[mp]
            outs.append((acc[:HEAD_W] / acc[HEAD_W:HEAD_W + 1]).T)
        o_ref[...] = _merge_maps(outs, lam_ref, gsub_ref, lam_init).astype(o_ref.dtype)


def _flash(qt, k, vt, lam_rows, gsub, n_rows, lam_init):
    n_maps = qt.shape[1]
    v_rows = vt.shape[1]
    tb = _pick_tile(n_rows, (FLASH_BLOCK, 512, 256, 128))
    nq = n_rows // tb
    qi_tab = np.concatenate([np.full(i + 1, i, np.int32) for i in range(nq)])
    ki_tab = np.concatenate([np.arange(i + 1, dtype=np.int32) for i in range(nq)])
    grid_spec = pltpu.PrefetchScalarGridSpec(
        num_scalar_prefetch=2,
        grid=(N_HEADS, len(qi_tab)),
        in_specs=[
            pl.BlockSpec((None, n_maps, HEAD_W + AUG_W, tb), lambda h, s, qt, kt: (h, 0, 0, qt[s])),
            pl.BlockSpec((None, tb, HEAD_W + AUG_W), lambda h, s, qt, kt: (h, kt[s], 0)),
            pl.BlockSpec((None, v_rows, tb), lambda h, s, qt, kt: (h, 0, kt[s])),
            pl.BlockSpec(lam_rows.shape, lambda h, s, qt, kt: (0, 0)),
            pl.BlockSpec(gsub.shape, lambda h, s, qt, kt: (0, 0)),
        ],
        out_specs=pl.BlockSpec((tb, HEAD_W), lambda h, s, qt, kt: (qt[s], h)),
        scratch_shapes=[
            pltpu.VMEM((n_maps, 1, tb), F32),
            pltpu.VMEM((n_maps, v_rows, tb), F32),
        ],
    )
    return pl.pallas_call(
        functools.partial(_flash_kernel, n_maps=n_maps, lam_init=lam_init),
        grid_spec=grid_spec,
        out_shape=jax.ShapeDtypeStruct((n_rows, SEC_W), BF16),
        compiler_params=_params("parallel", "arbitrary"),
        name=f"flash{n_maps}",
    )(jnp.asarray(qi_tab), jnp.asarray(ki_tab), qt, k, vt, lam_rows, gsub)


def _pagecum_kernel(x_ref, u_ref, ut_ref, wc_ref, wt_ref):
    hi, mid, lo = (a.astype(BF16) for a in _split3(x_ref[...]))
    u = u_ref[...]
    ut = ut_ref[...]
    wc_ref[...] = _dot(hi, u) + _dot(mid, u) + _dot(lo, u)
    wt_ref[...] = _dot(hi, ut) + _dot(mid, ut) + _dot(lo, ut)


def _pagecum(x, u, ut):
    r = x.shape[0]
    tr = _pick_tile(r, (512, 256, 128, 64, 32, 16, 8))
    spec = pl.BlockSpec((tr, PAGE_FLAT), lambda i: (i, 0))
    wspec = pl.BlockSpec((PAGE_FLAT, PAGE_FLAT), lambda i: (0, 0))
    return pl.pallas_call(
        _pagecum_kernel,
        grid=(r // tr,),
        in_specs=[spec, wspec, wspec],
        out_specs=[spec, spec],
        out_shape=[jax.ShapeDtypeStruct((r, PAGE_FLAT), F32)] * 2,
        compiler_params=_params("parallel"),
        name="pagecum",
    )(x, u, ut)


PAGES_PER_STEP = 16


def _softmax_update(scores, pv_fn, m_scr, l_scr, acc_scr):
    m_prev = m_scr[...]
    m_tile = functools.reduce(jnp.maximum, scores)
    m_new = jnp.maximum(m_prev, jnp.max(m_tile, axis=-1, keepdims=True))
    alpha = jnp.exp2(m_prev - m_new)
    ps = [jnp.exp2(sc - m_new) for sc in scores]
    row_sum = jnp.sum(ps[0], axis=-1, keepdims=True)
    for p in ps[1:]:
        row_sum = row_sum + jnp.sum(p, axis=-1, keepdims=True)
    l_scr[...] = alpha * l_scr[...] + row_sum
    acc_scr[...] = alpha * acc_scr[...] + pv_fn(ps)
    m_scr[...] = m_new


def _decode_forget_kernel(pt_ref, *refs, n_pages):
    pps = _pages_per_step(n_pages)
    q_ref = refs[0]
    k_refs = refs[1:1 + pps]
    v_refs = refs[1 + pps:1 + 2 * pps]
    wc_refs = refs[1 + 2 * pps:1 + 3 * pps]
    wt_refs = refs[1 + 3 * pps:1 + 4 * pps]
    cnew_ref, knew_ref, vnew_ref, o_ref, m_scr, l_scr, acc_scr, carry_scr = refs[1 + 4 * pps:]

    step = pl.program_id(1)
    n_steps = n_pages // pps
    q = q_ref[...]
    r = q.shape[0]

    @pl.when(step == 0)
    def _():
        m_scr[...] = jnp.full_like(m_scr, NEG_INF)
        l_scr[...] = jnp.zeros_like(l_scr)
        acc_scr[...] = jnp.zeros_like(acc_scr)
        carry_scr[...] = jnp.zeros_like(carry_scr)

    def scores(k, bias, causal):
        c = k.shape[0]
        sc = _dot_nt(q, k.astype(BF16)) + bias
        row = lax.broadcasted_iota(jnp.int32, (r, c), 0)
        col = lax.broadcasted_iota(jnp.int32, (r, c), 1)
        ok = (col & (N_HEADS - 1)) == ((row >> 3) & (N_HEADS - 1))
        if causal:
            ok = ok & ((col >> 3) <= (row & 7))
        return jnp.where(ok, sc, NEG_INF)

    carry = carry_scr[...]
    tiles = []
    for j in range(pps):
        tiles.append(scores(k_refs[j][...], (carry + wc_refs[j][...]) * (-LOG2E), False))
        carry = carry + wt_refs[j][...]
    carry_scr[...] = carry
    pv = lambda ps: sum(_dot(p.astype(BF16), v_refs[j][...].astype(BF16)) for j, p in enumerate(ps))
    _softmax_update(tiles, pv, m_scr, l_scr, acc_scr)

    @pl.when(step == n_steps - 1)
    def _():
        c = knew_ref.shape[0]
        bias = (carry_scr[:, :c] + cnew_ref[:, :c]) * (-LOG2E)
        tile = scores(knew_ref[...], bias, True)
        pv = lambda ps: _dot(ps[0].astype(BF16), vnew_ref[...].astype(BF16))
        _softmax_update([tile], pv, m_scr, l_scr, acc_scr)
        o_ref[...] = acc_scr[...] / l_scr[...]


def _decode_diff_kernel(pt_ref, *refs, n_pages, lam_init):
    pps = _pages_per_step(n_pages)
    q_ref = refs[0]
    kt_refs = refs[1:1 + pps]
    v_refs = refs[1 + pps:1 + 2 * pps]
    (slope_ref, knew_ref, vnew_ref, lam_ref, gsub_ref, o_ref,
     m_scr, l_scr, acc_scr, kpad_scr, vpad_scr) = refs[1 + 2 * pps:]

    step = pl.program_id(1)
    n_steps = n_pages // pps
    q = q_ref[...]
    r = q.shape[0]
    rows_per_head = r // N_HEADS
    n_new = knew_ref.shape[0]

    @pl.when(step == 0)
    def _():
        m_scr[...] = jnp.full_like(m_scr, NEG_INF)
        l_scr[...] = jnp.zeros_like(l_scr)
        acc_scr[...] = jnp.zeros_like(acc_scr)

    tok = lax.broadcasted_iota(jnp.int32, (1, PAGE), 1)

    def head_pv(ps, value_of):
        outs = []
        for h in range(N_HEADS):
            rs = slice(h * rows_per_head, (h + 1) * rows_per_head)
            outs.append(sum(_dot(p[rs].astype(BF16), value_of(j, h)) for j, p in enumerate(ps)))
        return jnp.concatenate(outs, axis=0)

    def page_scores(kt_ref):
        parts = []
        for c in range(SEC_W // MXU_DEPTH):
            rows = slice(c * r * MXU_DEPTH // SEC_W, (c + 1) * r * MXU_DEPTH // SEC_W)
            cols = slice(c * MXU_DEPTH, (c + 1) * MXU_DEPTH)
            parts.append(_dot(q_ref[rows, cols], kt_ref[cols, :].astype(BF16)))
        return jnp.concatenate(parts, axis=0)

    tiles = []
    for j in range(pps):
        pos = (tok + (step * pps + j) * PAGE).astype(F32)
        tiles.append(page_scores(kt_refs[j]) + slope_ref[...] * pos)
    page_value = lambda j, h: v_refs[j][pl.ds(h, PAGE, stride=N_HEADS), :].astype(BF16)
    _softmax_update(tiles, lambda ps: head_pv(ps, page_value), m_scr, l_scr, acc_scr)

    @pl.when(step == n_steps - 1)
    def _():
        kpad_scr[...] = jnp.zeros_like(kpad_scr)
        vpad_scr[...] = jnp.zeros_like(vpad_scr)
        kpad_scr[0:n_new, :] = knew_ref[...]
        vpad_scr[0:n_new, :] = vnew_ref[...]
        pos = (tok + n_pages * PAGE).astype(F32)
        sc = _dot_nt(q, kpad_scr[...].astype(BF16)) + slope_ref[...] * pos
        q_idx = lax.broadcasted_iota(jnp.int32, (r, PAGE), 0) & 7
        sc = jnp.where(tok <= q_idx, sc, NEG_INF)
        value_of = lambda j, h: vpad_scr[:, h * HEAD_W:(h + 1) * HEAD_W].astype(BF16)
        _softmax_update([sc], lambda ps: head_pv(ps, value_of), m_scr, l_scr, acc_scr)
        o = (acc_scr[...] / l_scr[...]).reshape(N_HEADS, 2, rows_per_head // 2, HEAD_W)
        outs = [o[:, mp].reshape(r // 2, HEAD_W) for mp in range(2)]
        o_ref[...] = _merge_maps(outs, lam_ref, gsub_ref, lam_init)


def _pages_per_step(n_pages):
    return _pick_tile(n_pages, (PAGES_PER_STEP, 4, 2, 1))


def _decode_specs(n_pages, layer):
    pps = _pages_per_step(n_pages)
    n_steps = n_pages // pps

    def cache_spec(j):
        return pl.BlockSpec((None, None, PAGE_FLAT, HEAD_W),
                            lambda bi, s, pt, j=j: (layer, pt[bi, s * pps + j], 0, 0))

    def cum_spec(j):
        return pl.BlockSpec((None, 1, PAGE_FLAT), lambda bi, s, pt, j=j: (pt[bi, s * pps + j], 0, 0))

    per_seq = lambda shape: pl.BlockSpec((None,) + shape, lambda bi, s, pt: (bi, 0, 0))
    whole2 = lambda a: pl.BlockSpec(a.shape, lambda bi, s, pt: (0, 0))
    return n_steps, cache_spec, cum_spec, per_seq, whole2


def _decode_forget(page_table, q, k_cache, v_cache, layer, knew, vnew, wc, wt, cnew):
    b, r, _ = q.shape
    n_pages = page_table.shape[1]
    pps = _pages_per_step(n_pages)
    n_steps, cache_spec, cum_spec, per_seq, _ = _decode_specs(n_pages, layer)
    c_new = knew.shape[1]
    pages = lambda spec: [spec(j) for j in range(pps)]
    in_specs = ([per_seq((r, HEAD_W))] + pages(cache_spec) + pages(cache_spec) + pages(cum_spec)
                + pages(cum_spec) + [per_seq((1, PAGE_FLAT)), per_seq((c_new, HEAD_W)),
                                     per_seq((c_new, HEAD_W))])
    args = [q] + [k_cache] * pps + [v_cache] * pps + [wc] * pps + [wt] * pps + [cnew, knew, vnew]
    grid_spec = pltpu.PrefetchScalarGridSpec(
        num_scalar_prefetch=1,
        grid=(b, n_steps),
        in_specs=in_specs,
        out_specs=per_seq((r, HEAD_W)),
        scratch_shapes=[
            pltpu.VMEM((r, 1), F32),
            pltpu.VMEM((r, 1), F32),
            pltpu.VMEM((r, HEAD_W), F32),
            pltpu.VMEM((1, PAGE_FLAT), F32),
        ],
    )
    return pl.pallas_call(
        functools.partial(_decode_forget_kernel, n_pages=n_pages),
        grid_spec=grid_spec,
        out_shape=jax.ShapeDtypeStruct((b, r, HEAD_W), F32),
        compiler_params=_params("parallel", "arbitrary"),
        name="decode_forget",
    )(page_table, *args)


def _decode_diff(page_table, q, kt_cache, v_cache, layer, knew, vnew, slope_rows, lam_rows, gsub,
                 lam_init):
    b, r, _ = q.shape
    n_pages = page_table.shape[1]
    pps = _pages_per_step(n_pages)
    n_steps, cache_spec, _, per_seq, whole2 = _decode_specs(n_pages, layer)
    n_new = knew.shape[1]
    pages = lambda spec: [spec(j) for j in range(pps)]
    in_specs = ([per_seq((r, SEC_W))] + pages(cache_spec) + pages(cache_spec)
                + [whole2(slope_rows), per_seq((n_new, SEC_W)), per_seq((n_new, SEC_W)),
                   whole2(lam_rows), whole2(gsub)])
    args = [q] + [kt_cache] * pps + [v_cache] * pps + [slope_rows, knew, vnew, lam_rows, gsub]
    grid_spec = pltpu.PrefetchScalarGridSpec(
        num_scalar_prefetch=1,
        grid=(b, n_steps),
        in_specs=in_specs,
        out_specs=per_seq((r // 2, HEAD_W)),
        scratch_shapes=[
            pltpu.VMEM((r, 1), F32),
            pltpu.VMEM((r, 1), F32),
            pltpu.VMEM((r, HEAD_W), F32),
            pltpu.VMEM((PAGE, SEC_W), F32),
            pltpu.VMEM((PAGE, SEC_W), F32),
        ],
    )
    return pl.pallas_call(
        functools.partial(_decode_diff_kernel, n_pages=n_pages, lam_init=lam_init),
        grid_spec=grid_spec,
        out_shape=jax.ShapeDtypeStruct((b, r // 2, HEAD_W), F32),
        compiler_params=_params("parallel", "arbitrary"),
        name="decode_diff",
    )(page_table, *args)


def _zmerge_kernel(h_ref, oa_ref, ob_ref, oc_ref, wa_ref, wb_ref, wc_ref,
                   wga_ref, wgb_ref, wgc_ref, bga_ref, bgb_ref, bgc_ref, z_ref):
    h = h_ref[...]
    ga = jax.nn.sigmoid(_dot(h, wga_ref[...]) + bga_ref[...])
    gb = jax.nn.sigmoid(_dot(h, wgb_ref[...]) + bgb_ref[...])
    gc = jax.nn.sigmoid(_dot(h, wgc_ref[...]) + bgc_ref[...])
    z = (ga * _dot(oa_ref[...], wa_ref[...]) + gb * _dot(ob_ref[...], wb_ref[...])
         + gc * _dot(oc_ref[...], wc_ref[...]))
    z_ref[...] = z.astype(z_ref.dtype)


def _zmerge(h, oa, ob, oc, wa, wb, wc, wg, bg, layer):
    m, d = h.shape
    tm = _pick_tile(m, (768, 512, 256))
    tn = _pick_tile(d, (512, 256))
    nn = d // tn
    rows = lambda w: pl.BlockSpec((tm, w), lambda i, j: (i, 0))
    wcol = lambda kdim, off: pl.BlockSpec((None, kdim, tn), lambda i, j, off=off: (layer, 0, j + off))
    bcol = lambda off: pl.BlockSpec((None, 1, tn), lambda i, j, off=off: (layer, 0, j + off))
    return pl.pallas_call(
        _zmerge_kernel,
        grid=(m // tm, nn),
        in_specs=[rows(d), rows(SEC_W), rows(SEC_W), rows(SEC_W),
                  wcol(SEC_W, 0), wcol(SEC_W, 0), wcol(SEC_W, 0),
                  wcol(d, 0), wcol(d, nn), wcol(d, 2 * nn),
                  bcol(0), bcol(nn), bcol(2 * nn)],
        out_specs=pl.BlockSpec((tm, tn), lambda i, j: (i, j)),
        out_shape=jax.ShapeDtypeStruct((m, d), BF16),
        compiler_params=_params("parallel", "arbitrary"),
        name="zmerge",
    )(h, oa, ob, oc, wa, wb, wc, wg, wg, wg, bg, bg, bg)


def _outproj_kernel(x_ref, z_ref, w_ref, o_ref):
    o_ref[...] = x_ref[...] + _dot(z_ref[...], w_ref[...])


def _outproj(x, z, w, layer):
    m, d = x.shape
    tm = _pick_tile(m, (768, 512, 256))
    tn = _pick_tile(d, (1024, 512))
    return pl.pallas_call(
        _outproj_kernel,
        grid=(m // tm, d // tn),
        in_specs=[pl.BlockSpec((tm, tn), lambda i, j: (i, j)),
                  pl.BlockSpec((tm, d), lambda i, j: (i, 0)),
                  pl.BlockSpec((None, d, tn), lambda i, j: (layer, 0, j))],
        out_specs=pl.BlockSpec((tm, tn), lambda i, j: (i, j)),
        out_shape=jax.ShapeDtypeStruct((m, d), F32),
        compiler_params=_params("parallel", "arbitrary"),
        name="outproj",
    )(x, z, w)


def _ffn_kernel(x_ref, g_ref, wg_ref, wu_ref, wd_ref, o_ref, h_scr):
    @pl.when(pl.program_id(1) == 0)
    def _():
        x = x_ref[...]
        h_scr[...] = _rms(x, g_ref[...]).astype(BF16)
        o_ref[...] = x

    h = h_scr[...]
    a = jax.nn.silu(_dot(h, wg_ref[...])) * _dot(h, wu_ref[...])
    o_ref[...] += _dot(a.astype(BF16), wd_ref[...])


def _ffn(x, g, wg, wu, wd, layer):
    m, d = x.shape
    f = wg.shape[2]
    tm = _pick_tile(m, (768, 512, 256))
    tf = _pick_tile(f, (512, 256))
    return pl.pallas_call(
        _ffn_kernel,
        grid=(m // tm, f // tf),
        in_specs=[pl.BlockSpec((tm, d), lambda i, j: (i, 0)),
                  pl.BlockSpec((1, d), lambda i, j: (0, 0)),
                  pl.BlockSpec((None, d, tf), lambda i, j: (layer, 0, j)),
                  pl.BlockSpec((None, d, tf), lambda i, j: (layer, 0, j)),
                  pl.BlockSpec((None, tf, d), lambda i, j: (layer, j, 0))],
        out_specs=pl.BlockSpec((tm, d), lambda i, j: (i, 0)),
        out_shape=jax.ShapeDtypeStruct((m, d), F32),
        scratch_shapes=[pltpu.VMEM((tm, d), BF16)],
        compiler_params=_params("parallel", "arbitrary"),
        name="ffn",
    )(x, g, wg, wu, wd)


def _page_cum_matrices():
    idx = np.arange(PAGE_FLAT)
    tok, head = idx >> 3, idx & 7
    same = head[:, None] == head[None, :]
    u = same & (tok[:, None] <= tok[None, :])
    return jnp.asarray(u, BF16), jnp.asarray(same, BF16)


def _mix_matrices(w_spatial, b_spatial, n_sample_tok):
    t = ROW_TILE
    tril = jnp.where(jnp.tril(jnp.ones((CHUNK, CHUNK), bool)), w_spatial, 0.0)
    eye_p = jnp.eye(t // CHUNK, dtype=F32)
    wm_p = jnp.einsum("ab,gts->gatbs", eye_p, tril).reshape(N_HEADS, t, t)
    bm_p = jnp.tile(b_spatial, (1, t // CHUNK))
    n = n_sample_tok
    eye_s = jnp.eye(t // n, dtype=F32)
    wm_s = jnp.einsum("ab,gts->gatbs", eye_s, tril[:, :n, :n]).reshape(N_HEADS, t, t)
    bm_s = jnp.tile(b_spatial[:, :n], (1, t // n))
    wm = jnp.stack([wm_p, wm_s]).astype(BF16)
    bm = jnp.stack([bm_p, bm_s])
    bm = jnp.pad(jnp.swapaxes(bm, 1, 2), ((0, 0), (0, 0), (0, HEAD_W - N_HEADS)))
    return wm, bm


def _lane_row(v, width=HEAD_W):
    v = v.reshape(1, -1)
    return jnp.pad(v, ((0, 0), (0, width - v.shape[1])))


def kernel(x_prompt, x_sample, cache_diff_k, cache_diff_v, cache_fox_k, cache_fox_v, cache_fox_logf,
           page_table, g_mix, w_in, b_forget, g_q_a, g_k_a, lam_q1, lam_k1, lam_q2, lam_k2, g_subln,
           g_q_b, g_k_b, g_sgu, w_spatial, b_spatial, w_branch_a, w_branch_b, w_branch_c, w_gate,
           b_gate, w_out, g_ffn, w_ffn_gate, w_ffn_up, w_ffn_down):
    depth = g_mix.shape[0]
    n_batch, seq, d_model = x_prompt.shape
    dec_b, dec_t, _ = x_sample.shape
    n_phys = cache_diff_k.shape[1]
    n_dec = dec_b * dec_t
    assert n_batch == 1 and d_model == N_HEADS * 2 * HEAD_W
    assert seq % ROW_TILE == 0 and n_dec == ROW_TILE and dec_t == 8
    assert cache_diff_k.shape[2] == PAGE
    m = seq + n_dec

    x = jnp.concatenate([x_prompt.reshape(seq, d_model), x_sample.reshape(n_dec, d_model)], axis=0)

    cdk = jnp.transpose(cache_diff_k, (0, 1, 3, 4, 5, 2)).reshape(depth, n_phys, PAGE_FLAT, PAGE)
    cdv = cache_diff_v.reshape(depth, n_phys, PAGE_FLAT, HEAD_W)
    cfk = cache_fox_k.reshape(depth, n_phys, PAGE_FLAT, HEAD_W)
    cfv = cache_fox_v.reshape(depth, n_phys, PAGE_FLAT, HEAD_W)
    cfl = cache_fox_logf.reshape(depth, n_phys, PAGE_FLAT)

    slopes = np.exp2(-8.0 * np.arange(1, N_HEADS + 1) / N_HEADS).astype(np.float32)
    slope_row = jnp.asarray(np.pad(slopes, (0, HEAD_W - N_HEADS)).reshape(1, HEAD_W))
    slope_rows = jnp.asarray(np.broadcast_to(np.repeat(slopes * np.float32(LOG2E), 2 * dec_t)[:, None],
                                             (2 * N_HEADS * dec_t, PAGE)))
    head_block = jnp.asarray((np.arange(SEC_W) // HEAD_W)[None, :] == np.arange(N_HEADS)[:, None]
                             ).reshape(1, N_HEADS, 1, 1, SEC_W)
    ltri = jnp.asarray(np.tril(np.ones((ROW_TILE, ROW_TILE), np.float32)), BF16)
    u_cum, u_tot = _page_cum_matrices()
    widths = np.cumsum([SEC_W] * 6 + [N_HEADS])

    wa16, wb16, wc16 = w_branch_a.astype(BF16), w_branch_b.astype(BF16), w_branch_c.astype(BF16)
    wg16, wo16 = w_gate.astype(BF16), w_out.astype(BF16)
    wfg16, wfu16, wfd16 = w_ffn_gate.astype(BF16), w_ffn_up.astype(BF16), w_ffn_down.astype(BF16)

    rows_prompt, rows_sample = [], []
    cache_rows = ()
    for layer in range(depth):
        lam_init = 0.8 - 0.6 * math.exp(-0.3 * layer)
        w_l = w_in[layer]
        w_main = jnp.concatenate([w_l[:, :widths[5]], w_l[:, widths[6]:]], axis=1).astype(BF16)
        w_f = jnp.pad(w_l[:, widths[5]:widths[6]], ((0, 0), (0, AUG_W - N_HEADS))).astype(BF16)
        p, pf, h = _proj_in(x, g_mix[layer].reshape(1, d_model), w_main, w_f)

        wm, bm = _mix_matrices(w_spatial[layer], b_spatial[layer], dec_t)
        lam_rows = jnp.pad(jnp.stack([lam_q1[layer], lam_k1[layer], lam_q2[layer], lam_k2[layer]]),
                           ((0, 4), (0, HEAD_W - MAP_W)))
        gsub = g_subln[layer].reshape(1, HEAD_W)
        (kan, kbn, logf, vcn, oc, qa_o, ka_o, va_o, qb_o, kb_o, vb_o, *cache_rows) = _post(
            p, pf,
            jnp.tile(g_q_a[layer], 2).reshape(1, HEAD_W), jnp.tile(g_k_a[layer], 2).reshape(1, HEAD_W),
            g_q_b[layer].reshape(1, HEAD_W), g_k_b[layer].reshape(1, HEAD_W),
            g_sgu[layer].reshape(1, SEC_W), _lane_row(b_forget[layer]), slope_row,
            wm, bm, ltri, seq // ROW_TILE, depth, layer, cache_rows)
        va = p[seq:, 2 * SEC_W:3 * SEC_W]
        vb = p[seq:, 5 * SEC_W:6 * SEC_W]

        oa_p = _flash(qa_o, ka_o, va_o, lam_rows, gsub, seq, lam_init)
        ob_p = _flash(qb_o, kb_o, vb_o, lam_rows, gsub, seq, lam_init)

        kan_s, kbn_s = kan[seq:], kbn[seq:]
        to_rows = lambda a: a.reshape(dec_b, dec_t * N_HEADS, HEAD_W)
        per_seq = lambda a: a.reshape(dec_b, dec_t, SEC_W)
        q_hm = jnp.transpose(qa_o[:, :, :HEAD_W, seq:].reshape(N_HEADS, 2, HEAD_W, dec_b, dec_t),
                             (3, 0, 1, 4, 2))
        q_a = jnp.where(head_block, jnp.tile(q_hm, (1, 1, 1, 1, N_HEADS)), 0).reshape(
            dec_b, 2 * N_HEADS * dec_t, SEC_W)
        q_b = jnp.transpose(qb_o[:, 0, :HEAD_W, seq:].reshape(N_HEADS, HEAD_W, dec_b, dec_t),
                            (2, 0, 3, 1)).reshape(dec_b, N_HEADS * dec_t, HEAD_W)
        oa_s = _decode_diff(page_table, q_a, cdk, cdv, layer, per_seq(kan_s), per_seq(va), slope_rows,
                            lam_rows, gsub, lam_init)
        wc, wt = _pagecum(cfl[layer], u_cum, u_tot)
        lf_new = jnp.pad(logf[seq:, :N_HEADS].reshape(dec_b, dec_t * N_HEADS),
                         ((0, 0), (0, PAGE_FLAT - dec_t * N_HEADS)))
        cnew, _ = _pagecum(lf_new, u_cum, u_tot)
        ob_s = _decode_forget(page_table, q_b, cfk, cfv, layer, to_rows(kbn_s), to_rows(vb),
                              wc.reshape(n_phys, 1, PAGE_FLAT), wt.reshape(n_phys, 1, PAGE_FLAT),
                              cnew.reshape(dec_b, 1, PAGE_FLAT))
        from_rows = lambda o: jnp.transpose(o.reshape(dec_b, N_HEADS, dec_t, HEAD_W),
                                            (0, 2, 1, 3)).reshape(n_dec, SEC_W).astype(BF16)
        oa = jnp.concatenate([oa_p, from_rows(oa_s)], axis=0)
        ob = jnp.concatenate([ob_p, from_rows(ob_s)], axis=0)

        z = _zmerge(h, oa, ob, oc, wa16, wb16, wc16, wg16, b_gate.reshape(depth, 1, -1), layer)
        x = _outproj(x, z, wo16, layer)
        x = _ffn(x, g_ffn[layer].reshape(1, d_model), wfg16, wfu16, wfd16, layer)

        lf = logf[:, :N_HEADS]
        rows_prompt.append((lf[:seq],))
        rows_sample.append((kan_s, va, kbn_s, vb, lf[seq:], vcn[seq:]))

    stack = lambda rows, i, shape: jnp.stack([r[i] for r in rows], axis=0).reshape((depth,) + shape)
    p_shape, s_shape = (n_batch, seq), (dec_b, dec_t)
    kan_pt, va_p, kbn_p, vb_p = cache_rows
    kan_p = jnp.transpose(kan_pt, (0, 2, 1))
    return (
        x[:seq].reshape(n_batch, seq, d_model),
        x[seq:].reshape(dec_b, dec_t, d_model),
        kan_p.reshape((depth,) + p_shape + (N_HEADS, 2, MAP_W)),
        va_p.reshape((depth,) + p_shape + (N_HEADS, HEAD_W)),
        kbn_p.reshape((depth,) + p_shape + (N_HEADS, HEAD_W)),
        vb_p.reshape((depth,) + p_shape + (N_HEADS, HEAD_W)),
        stack(rows_prompt, 0, p_shape + (N_HEADS,)),
        stack(rows_sample, 0, s_shape + (N_HEADS, 2, MAP_W)),
        stack(rows_sample, 1, s_shape + (N_HEADS, HEAD_W)),
        stack(rows_sample, 2, s_shape + (N_HEADS, HEAD_W)),
        stack(rows_sample, 3, s_shape + (N_HEADS, HEAD_W)),
        stack(rows_sample, 4, s_shape + (N_HEADS,)),
        stack(rows_sample, 5, s_shape + (SEC_W,)),
    )
```
